```python
import jax, jax.numpy as jnp
from jax import lax
import numpy as np

D_MODEL = 1024
BATCH = 8
SEQ = 4096
DEPTH = 2
DEC_BATCH = 32
DEC_SEQ = 4
PAST_LEN = 16384
PAGE_SIZE = 128

N_META = 16
HEAD_DIM = 64
W_A = D_MODEL // 2
W_B = D_MODEL // 2
H_B = W_B // HEAD_DIM
W_C = D_MODEL // 2
H_C = W_C // HEAD_DIM
CONV_W = 3
LORA_W = 64
LORA_A = 64
LORA_G = 128
D_FF = 4 * D_MODEL
Q_BLOCK = 128
N_BRANCH = 3
C_COLS = 3 * W_C + LORA_W + LORA_A + LORA_G
IN_COLS = 3 * W_A + 3 * W_B + C_COLS + N_BRANCH * D_MODEL
NORM_EPS = 1e-6
LNX_EPS = 64e-5
POOL_NUM = 5
POOL_DEN = 4

kernel_name = "hybrid_gated_conv_stickbreak_rwkv7_step"


def _rms_norm(x, g):
    xf = x.astype(jnp.float32)
    xf = xf * lax.rsqrt(jnp.mean(xf * xf, axis=-1, keepdims=True) + NORM_EPS)
    return (xf * g.astype(jnp.float32)).astype(x.dtype)


def _stick_breaking(q, k, v, bias, q_pos, k_pos):
    B, Tq, H, Dh = q.shape
    blk = min(Q_BLOCK, Tq)
    nb = -(-Tq // blk)
    pad = nb * blk - Tq
    qp = jnp.pad(q, ((0, 0), (0, pad), (0, 0), (0, 0)))
    qpos = jnp.pad(q_pos, (0, pad), mode='edge')
    qb = qp.reshape(B, nb, blk, H, Dh).transpose(1, 0, 2, 3, 4)
    pb = qpos.reshape(nb, blk)
    scale = 1.0 / np.sqrt(HEAD_DIM)
    b_h = bias.astype(jnp.float32)[None, :, None, None]

    def one_block(args):
        qi, pi = args
        z = jnp.einsum('bqhd,bkhd->bhqk', qi, k, preferred_element_type=jnp.float32) * scale + b_h
        mask = k_pos[None, :] < pi[:, None]
        log_1mb = jnp.where(mask, jax.nn.log_sigmoid(-z), 0.0)
        tail = lax.cumsum(log_1mb, axis=3, reverse=True) - log_1mb
        att = jnp.where(mask, jnp.exp(jax.nn.log_sigmoid(z) + tail), 0.0)
        return jnp.einsum('bhqk,bkhd->bqhd', att.astype(v.dtype), v,
                          preferred_element_type=jnp.float32).astype(q.dtype)

    o = lax.map(one_block, (qb, pb))
    return o.transpose(1, 0, 2, 3, 4).reshape(B, nb * blk, H, Dh)[:, :Tq]


def _rwkv7_scan(S0, r, decay, k, v, kk, a):
    xs = tuple(jnp.moveaxis(t.astype(jnp.float32), 1, 0) for t in (r, decay, k, v, kk, a))

    def step(S, inp):
        r_t, w_t, k_t, v_t, kk_t, a_t = inp
        sa = jnp.einsum('bhij,bhj->bhi', S, -kk_t)
        S = (S * w_t[:, :, None, :] + sa[..., None] * (kk_t * a_t)[:, :, None, :]
             + v_t[..., None] * k_t[:, :, None, :])
        return S, jnp.einsum('bhij,bhj->bhi', S, r_t)

    S, y = lax.scan(step, S0.astype(jnp.float32), xs)
    return S, jnp.moveaxis(y, 0, 1)


def _layer(x, conv_buf, shift_buf, wkv, k_past, v_past, p):
    (g_mix, w_in, conv_w, q_g, k_g, sb_b, mu_c, w0, w_w2, a0, w_a2, w_g2, k_k, k_a, r_k,
     lnx_g, lnx_b, w_br_a, w_br_b, w_br_c, w_o, g_mlp, w_up, w_down) = p
    B, T, _ = x.shape
    xn = _rms_norm(x, g_mix)
    proj = xn @ w_in
    s1 = 3 * W_A
    s2 = s1 + 3 * W_B
    s3 = s2 + C_COLS
    pa, pb, pc, pg = jnp.split(proj, [s1, s2, s3], axis=-1)

    gb, gc, h = jnp.split(pa, 3, axis=-1)
    u = gc * h
    up = jnp.concatenate([conv_buf.astype(u.dtype), u], axis=1)
    conv = conv_w[0] * up[:, 0:T]
    for i in range(1, CONV_W):
        conv = conv + conv_w[i] * up[:, i:i + T]
    y_a = (gb * conv) @ w_br_a
    new_conv = up[:, -(CONV_W - 1):]

    q, kb, vb = jnp.split(pb, 3, axis=-1)
    q = _rms_norm(q.reshape(B, T, H_B, HEAD_DIM), q_g)
    kb = _rms_norm(kb.reshape(B, T, H_B, HEAD_DIM), k_g)
    vb = vb.reshape(B, T, H_B, HEAD_DIM)
    P = k_past.shape[1]
    k_all = jnp.concatenate([k_past.astype(kb.dtype), kb], axis=1)
    v_all = jnp.concatenate([v_past.astype(vb.dtype), vb], axis=1)
    k_pos = jnp.arange(P + T, dtype=jnp.int32)
    q_pos = P + jnp.arange(T, dtype=jnp.int32)
    o = _stick_breaking(q, k_all, v_all, sb_b, q_pos, k_pos)
    y_b = o.reshape(B, T, W_B) @ w_br_b

    prev = jnp.concatenate([shift_buf.astype(pc.dtype)[:, None], pc[:, :-1]], axis=1)
    pcs = pc + (prev - pc) * mu_c
    new_shift = pc[:, -1]
    r, kc, vc, wl, al, gl = jnp.split(
        pcs, [W_C, 2 * W_C, 3 * W_C, 3 * W_C + LORA_W, 3 * W_C + LORA_W + LORA_A], axis=-1)
    w_log = -jax.nn.softplus(-(w0 + jnp.tanh(wl) @ w_w2).astype(jnp.float32)) - 0.5
    decay = jnp.exp(-jnp.exp(w_log))
    a = jax.nn.sigmoid((a0 + al @ w_a2).astype(jnp.float32))
    g = jax.nn.sigmoid(gl) @ w_g2
    kk = (kc * k_k).astype(jnp.float32).reshape(B, T, H_C, HEAD_DIM)
    kk = kk / jnp.maximum(jnp.sqrt(jnp.sum(kk * kk, axis=-1, keepdims=True)), 1e-12)
    kmod = kc.astype(jnp.float32) * (1.0 + (a - 1.0) * k_a)
    rh = r.astype(jnp.float32).reshape(B, T, H_C, HEAD_DIM)
    kh = kmod.reshape(B, T, H_C, HEAD_DIM)
    vh = vc.astype(jnp.float32).reshape(B, T, H_C, HEAD_DIM)
    S_new, yc = _rwkv7_scan(wkv, rh, decay.reshape(B, T, H_C, HEAD_DIM), kh, vh, kk,
                            a.reshape(B, T, H_C, HEAD_DIM))
    mu = jnp.mean(yc, axis=-1, keepdims=True)
    var = jnp.mean(jnp.square(yc - mu), axis=-1, keepdims=True)
    yc = ((yc - mu) * lax.rsqrt(var + LNX_EPS)).reshape(B, T, W_C) * lnx_g + lnx_b
    bonus = jnp.sum(rh * kh * r_k, axis=-1, keepdims=True) * vh
    yc = (yc + bonus.reshape(B, T, W_C)) * g.astype(jnp.float32)
    y_c = yc.astype(x.dtype) @ w_br_c

    gates = jax.nn.sigmoid(pg).reshape(B, T, N_BRANCH, D_MODEL)
    mixed = gates[:, :, 0] * y_a + gates[:, :, 1] * y_b + gates[:, :, 2] * y_c
    x = x + mixed @ w_o
    xn2 = _rms_norm(x, g_mlp)
    x = x + jnp.square(jax.nn.relu(xn2 @ w_up)) @ w_down
    return x, new_conv, new_shift, S_new, kb, vb


def setup_inputs(seed: int = 0) -> dict:
    key = jax.random.key(seed)
    ks = iter(jax.random.split(key, 40))
    f32 = jnp.float32
    n_pages = PAST_LEN // PAGE_SIZE
    n_used = DEC_BATCH * n_pages
    n_pool = (n_used * POOL_NUM) // POOL_DEN

    def nrm(shape, scale=1.0):
        return jax.random.normal(next(ks), shape, f32) * scale

    def gain(shape):
        return 1.0 + nrm(shape, 0.02)

    perm = jax.random.permutation(next(ks), n_pool)[:n_used]
    return {
        "x_prompt": nrm((BATCH, SEQ, D_MODEL)),
        "x_sample": nrm((DEC_BATCH, DEC_SEQ, D_MODEL)),
        "cache_k": nrm((DEPTH, n_pool, PAGE_SIZE, H_B, HEAD_DIM)),
        "cache_v": nrm((DEPTH, n_pool, PAGE_SIZE, H_B, HEAD_DIM)),
        "state_conv": nrm((DEPTH, DEC_BATCH, CONV_W - 1, W_A)),
        "state_shift": nrm((DEPTH, DEC_BATCH, C_COLS)),
        "state_wkv": nrm((DEPTH, DEC_BATCH, H_C, HEAD_DIM, HEAD_DIM)),
        "page_table": perm.reshape(DEC_BATCH, n_pages).astype(jnp.int32),
        "meta_tokens": nrm((N_META, D_MODEL)),
        "norm_mix_g": gain((DEPTH, D_MODEL)),
        "w_in": nrm((DEPTH, D_MODEL, IN_COLS), D_MODEL ** -0.5),
        "conv_w": nrm((DEPTH, CONV_W, W_A), CONV_W ** -0.5),
        "q_norm_g": gain((DEPTH, HEAD_DIM)),
        "k_norm_g": gain((DEPTH, HEAD_DIM)),
        "sb_bias": -float(np.log(PAST_LEN + SEQ)) + nrm((DEPTH, H_B), 0.1),
        "mu_c": jax.random.uniform(next(ks), (DEPTH, C_COLS), f32),
        "w0": jax.random.uniform(next(ks), (DEPTH, W_C), f32, -3.0, 1.0),
        "w_w2": nrm((DEPTH, LORA_W, W_C), 0.1 * LORA_W ** -0.5),
        "a0": nrm((DEPTH, W_C), 0.1),
        "w_a2": nrm((DEPTH, LORA_A, W_C), 0.1 * LORA_A ** -0.5),
        "w_g2": nrm((DEPTH, LORA_G, W_C), LORA_G ** -0.5),
        "k_k": 0.85 + nrm((DEPTH, W_C), 0.05),
        "k_a": 1.0 + nrm((DEPTH, W_C), 0.05),
        "r_k": nrm((DEPTH, H_C, HEAD_DIM), 0.1),
        "lnx_g": gain((DEPTH, W_C)),
        "lnx_b": nrm((DEPTH, W_C), 0.02),
        "w_br_a": nrm((DEPTH, W_A, D_MODEL), W_A ** -0.5),
        "w_br_b": nrm((DEPTH, W_B, D_MODEL), W_B ** -0.5),
        "w_br_c": nrm((DEPTH, W_C, D_MODEL), W_C ** -0.5),
        "w_o": nrm((DEPTH, D_MODEL, D_MODEL), D_MODEL ** -0.5),
        "norm_mlp_g": gain((DEPTH, D_MODEL)),
        "w_up": nrm((DEPTH, D_MODEL, D_FF), D_MODEL ** -0.5),
        "w_down": nrm((DEPTH, D_FF, D_MODEL), D_FF ** -0.5),
    }


def reference(x_prompt, x_sample, cache_k, cache_v, state_conv, state_shift, state_wkv,
              page_table, meta_tokens, norm_mix_g, w_in, conv_w, q_norm_g, k_norm_g, sb_bias,
              mu_c, w0, w_w2, a0, w_a2, w_g2, k_k, k_a, r_k, lnx_g, lnx_b, w_br_a, w_br_b,
              w_br_c, w_o, norm_mlp_g, w_up, w_down):
    B = x_prompt.shape[0]
    DB = x_sample.shape[0]
    dt = x_prompt.dtype
    meta = jnp.broadcast_to(meta_tokens.astype(dt)[None], (B, N_META, D_MODEL))
    xp = jnp.concatenate([meta, x_prompt], axis=1)
    xs = x_sample
    kp_l, vp_l, ks_l, vs_l, cp_l, cs_l, sp_l, ss_l, wp_l, ws_l = ([] for _ in range(10))
    for l in range(DEPTH):
        p = (norm_mix_g[l], w_in[l], conv_w[l], q_norm_g[l], k_norm_g[l], sb_bias[l], mu_c[l],
             w0[l], w_w2[l], a0[l], w_a2[l], w_g2[l], k_k[l], k_a[l], r_k[l], lnx_g[l],
             lnx_b[l], w_br_a[l], w_br_b[l], w_br_c[l], w_o[l], norm_mlp_g[l], w_up[l],
             w_down[l])
        xp, cp, sp, wp, kp, vp = _layer(
            xp, jnp.zeros((B, CONV_W - 1, W_A), dt), jnp.zeros((B, C_COLS), dt),
            jnp.zeros((B, H_C, HEAD_DIM, HEAD_DIM), jnp.float32),
            jnp.zeros((B, 0, H_B, HEAD_DIM), dt), jnp.zeros((B, 0, H_B, HEAD_DIM), dt), p)
        k_past = cache_k[l][page_table].reshape(DB, -1, H_B, HEAD_DIM)
        v_past = cache_v[l][page_table].reshape(DB, -1, H_B, HEAD_DIM)
        xs, cs, ss, ws, ksn, vsn = _layer(xs, state_conv[l], state_shift[l], state_wkv[l],
                                          k_past, v_past, p)
        kp_l.append(kp); vp_l.append(vp); ks_l.append(ksn); vs_l.append(vsn)
        cp_l.append(cp); cs_l.append(cs); sp_l.append(sp); ss_l.append(ss)
        wp_l.append(wp); ws_l.append(ws)
    return (xp[:, N_META:], xs,
            jnp.stack(kp_l), jnp.stack(vp_l), jnp.stack(ks_l), jnp.stack(vs_l),
            jnp.stack(cp_l), jnp.stack(cs_l), jnp.stack(sp_l), jnp.stack(ss_l),
            jnp.stack(wp_l), jnp.stack(ws_l))
```

```python
import functools

import numpy as np
import jax
import jax.numpy as jnp
from jax import lax
from jax.experimental import pallas as pl
from jax.experimental.pallas import tpu as pltpu

HEAD_DIM = 64
N_META = 16
CONV_W = 3
LORA_W = 64
LORA_A = 64
LORA_G = 128
NORM_EPS = 1e-6
LNX_EPS = 64e-5
LANES = 128
RWKV_CHUNK = 64
PAGE = 128
VMEM_LIMIT = 56 * 1024 * 1024

f32 = jnp.float32
bf16 = jnp.bfloat16


def _bf(x):
    return x.astype(bf16)


def _dot(a, b):
    return jnp.dot(a, b, preferred_element_type=f32)


def _dot_nt(a, b):
    return lax.dot_general(a, b, (((1,), (1,)), ((), ())), preferred_element_type=f32)


def _dot_tn(a, b):
    return lax.dot_general(a, b, (((0,), (0,)), ((), ())), preferred_element_type=f32)


def _split2(x):
    hi = x.astype(bf16)
    lo = (x - hi.astype(f32)).astype(bf16)
    return hi, lo


def _split3(x):
    hi = x.astype(bf16)
    r1 = x - hi.astype(f32)
    mid = r1.astype(bf16)
    lo = (r1 - mid.astype(f32)).astype(bf16)
    return hi, mid, lo


def _dot_sel(x, sel):
    hi, lo = _split2(x)
    return _dot(hi, sel) + _dot(lo, sel)


def _dot3(a, b, fn=_dot):
    ah, al = _split2(a)
    bh, bl = _split2(b)
    return fn(ah, bh) + (fn(ah, bl) + fn(al, bh))


def _rms(x, g):
    ms = jnp.mean(x * x, axis=-1, keepdims=True)
    return x * lax.rsqrt(ms + NORM_EPS) * g


def _segsum(x, bd):
    w = bd.shape[0]
    parts = [_dot_sel(x[:, s:s + w], bd) for s in range(0, x.shape[1], w)]
    return parts[0] if len(parts) == 1 else jnp.concatenate(parts, axis=1)


def _neg_softplus(z):
    return -(jnp.maximum(z, 0.0) + jnp.log1p(jnp.exp(-jnp.abs(z))))


def _pick_tile(n, target, mult):
    best = None
    for t in range(mult, min(n, target) + 1, mult):
        if n % t == 0:
            best = t
    if best is None:
        assert n <= target, (n, target, mult)
        best = n
    return best


def _block_diag_ones(n, blk):
    i = np.arange(n)
    return jnp.asarray((i[:, None] // blk) == (i[None, :] // blk), dtype=bf16)


def _full(shape):
    nd = len(shape)
    return pl.BlockSpec(shape, lambda *_: (0,) * nd)


def _params(sem):
    return pltpu.CompilerParams(dimension_semantics=sem, vmem_limit_bytes=VMEM_LIMIT)


def _shift_down(v, k, prev, tpos):
    out = pltpu.roll(v, k, 0)
    for j in range(k):
        out = jnp.where(tpos == j, prev[j], out)
    return out


def _mixer_a_kernel(x_ref, g_ref, wa_ref, wg_ref, cw_ref, wbr_ref, st_ref, y_ref, ns_ref, carry_ref,
                    *, bb, tt, wa, t_valid):
    ti = pl.program_id(1)

    @pl.when(ti == 0)
    def _():
        carry_ref[...] = st_ref[...]

    d = x_ref.shape[-1]
    x = x_ref[...].reshape(bb * tt, d)
    xn = _bf(_rms(x, g_ref[...]))
    pa = _dot(xn, wa_ref[...])
    gb, gc, h = pa[:, :wa], pa[:, wa:2 * wa], pa[:, 2 * wa:]
    u = gc * h
    tpos = lax.broadcasted_iota(jnp.int32, (bb, tt, wa), 1).reshape(bb * tt, wa)
    carry = carry_ref[...]
    c0 = jnp.broadcast_to(carry[:, 0:1, :], (bb, tt, wa)).reshape(bb * tt, wa)
    c1 = jnp.broadcast_to(carry[:, 1:2, :], (bb, tt, wa)).reshape(bb * tt, wa)
    u1 = _shift_down(u, 1, (c1,), tpos)
    u2 = _shift_down(u, 2, (c0, c1), tpos)
    cw = cw_ref[...]
    conv = cw[0:1] * u2 + cw[1:2] * u1 + cw[2:3] * u
    ya = _dot(_bf(gb * conv), wbr_ref[...])
    gate = jax.nn.sigmoid(_dot(xn, wg_ref[...]))
    y_ref[...] = (gate * ya).reshape(bb, tt, d)
    u3 = u.reshape(bb, tt, wa)
    carry_ref[...] = u3[:, tt - 2:tt, :]

    lo = t_valid - 2
    @pl.when(ti == lo // tt)
    def _():
        ns_ref[...] = u3[:, lo % tt:lo % tt + 2, :]


def _mixer_a(x, g, w_a, w_ga, conv_w, w_br, state, *, t_valid, bb, tt):
    bg, tp, d = x.shape
    wa = w_br.shape[0]
    assert (t_valid - 2) // tt == (t_valid - 1) // tt
    grid = (bg // bb, tp // tt)
    kern = functools.partial(_mixer_a_kernel, bb=bb, tt=tt, wa=wa, t_valid=t_valid)
    return pl.pallas_call(
        kern,
        grid=grid,
        in_specs=[
            pl.BlockSpec((bb, tt, d), lambda b, t: (b, t, 0)),
            _full((1, d)), _full(w_a.shape), _full(w_ga.shape), _full(conv_w.shape), _full(w_br.shape),
            pl.BlockSpec((bb, 2, wa), lambda b, t: (b, 0, 0)),
        ],
        out_specs=[
            pl.BlockSpec((bb, tt, d), lambda b, t: (b, t, 0)),
            pl.BlockSpec((bb, 2, wa), lambda b, t: (b, 0, 0)),
        ],
        out_shape=[jax.ShapeDtypeStruct((bg, tp, d), f32), jax.ShapeDtypeStruct((bg, 2, wa), f32)],
        scratch_shapes=[pltpu.VMEM((bb, 2, wa), f32)],
        compiler_params=_params(("arbitrary", "arbitrary")),
        name="mixer_a",
    )(x, g, w_a, w_ga, conv_w, w_br, state)


def _proj_b_kernel(x_ref, g_ref, wb_ref, qg_ref, kg_ref, bd_ref, q16_ref, k16_ref, v16_ref, k_ref, v_ref, *, wb):
    xn = _bf(_rms(x_ref[...], g_ref[...]))
    pb = _dot(xn, wb_ref[...])
    q, k, v = pb[:, :wb], pb[:, wb:2 * wb], pb[:, 2 * wb:]
    bd = bd_ref[...]
    inv = 1.0 / HEAD_DIM
    qn = q * lax.rsqrt(_segsum(q * q, bd) * inv + NORM_EPS) * qg_ref[...]
    kn = k * lax.rsqrt(_segsum(k * k, bd) * inv + NORM_EPS) * kg_ref[...]
    q16_ref[...] = _bf(qn * (HEAD_DIM ** -0.5))
    k16_ref[...] = _bf(kn)
    v16_ref[...] = _bf(v)
    k_ref[...] = kn
    v_ref[...] = v


def _proj_b(x2, g, w_b, qg, kg, bd, *, tm):
    n, d = x2.shape
    wb = w_b.shape[1] // 3
    row = lambda i: (i, 0)
    kern = functools.partial(_proj_b_kernel, wb=wb)
    return pl.pallas_call(
        kern,
        grid=(n // tm,),
        in_specs=[pl.BlockSpec((tm, d), row), _full((1, d)), _full(w_b.shape), _full((1, wb)), _full((1, wb)),
                  _full(bd.shape)],
        out_specs=[pl.BlockSpec((tm, wb), row)] * 5,
        out_shape=[jax.ShapeDtypeStruct((n, wb), bf16)] * 3 + [jax.ShapeDtypeStruct((n, wb), f32)] * 2,
        compiler_params=_params(("arbitrary",)),
        name="proj_b",
    )(x2, g, w_b, qg, kg, bd)


def _sb_block(qh, kb, vb, bias, tri, c, acc, mask):
    z = _dot_nt(qh, kb) + bias
    l = _neg_softplus(z)
    if mask is not None:
        l = jnp.where(mask, l, 0.0)
    incl = _dot_sel(l, tri)
    p = jnp.exp(z + incl + c)
    if mask is not None:
        p = jnp.where(mask, p, 0.0)
    acc = acc + _dot(_bf(p), vb)
    c = c + incl[:, 0:1]
    return c, acc


def _attn_prompt_kernel(bias_ref, q_ref, k_ref, v_ref, tri_ref, o_ref, *, tq):
    hp = pl.program_id(1)
    qi = pl.program_id(2)
    q2 = q_ref[0]
    tri = tri_ref[...]
    lane = lax.broadcasted_iota(jnp.int32, (tq, LANES), 1)
    row = lax.broadcasted_iota(jnp.int32, (tq, tq), 0)
    col = lax.broadcasted_iota(jnp.int32, (tq, tq), 1)
    causal = col < row
    start = pl.multiple_of(qi * tq, tq)
    outs = []
    for j in range(2):
        head_lanes = (lane < HEAD_DIM) if j == 0 else (lane >= HEAD_DIM)
        qh = jnp.where(head_lanes, q2, jnp.zeros_like(q2))
        bias = bias_ref[2 * hp + j]
        c0 = jnp.zeros((tq, 1), f32)
        a0 = jnp.zeros((tq, LANES), f32)
        c, acc = _sb_block(qh, k_ref[0, pl.ds(start, tq), :], v_ref[0, pl.ds(start, tq), :], bias, tri, c0, a0,
                           causal)

        def body(n, carry, qh=qh, bias=bias):
            s = pl.multiple_of((qi - 1 - n) * tq, tq)
            return _sb_block(qh, k_ref[0, pl.ds(s, tq), :], v_ref[0, pl.ds(s, tq), :], bias, tri, carry[0],
                             carry[1], None)

        c, acc = lax.fori_loop(0, qi, body, (c, acc))
        outs.append(acc)
    o_ref[0] = jnp.where(lane < HEAD_DIM, outs[0], outs[1])


def _attn_prompt(q16, k16, v16, bias, tri, *, tq):
    bg, tp, wb = q16.shape
    grid = (bg, wb // LANES, tp // tq)
    kern = functools.partial(_attn_prompt_kernel, tq=tq)
    return pl.pallas_call(
        kern,
        grid=grid,
        in_specs=[
            pl.BlockSpec(memory_space=pltpu.SMEM),
            pl.BlockSpec((1, tq, LANES), lambda b, h, i: (b, i, h)),
            pl.BlockSpec((1, tp, LANES), lambda b, h, i: (b, 0, h)),
            pl.BlockSpec((1, tp, LANES), lambda b, h, i: (b, 0, h)),
            _full(tri.shape),
        ],
        out_specs=pl.BlockSpec((1, tq, LANES), lambda b, h, i: (b, i, h)),
        out_shape=jax.ShapeDtypeStruct((bg, tp, wb), f32),
        compiler_params=_params(("arbitrary", "arbitrary", "arbitrary")),
        name="attn_prompt",
    )(bias, q16, k16, v16, tri)


def _attn_sample_kernel(pt_ref, qr_ref, kn_ref, vn_ref, bias_ref, tri_ref, hm_ref, *rest, n_group, t_new, n_heads):
    kp_refs = rest[:n_group]
    vp_refs = rest[n_group:2 * n_group]
    o_ref, acc_ref, c_ref = rest[2 * n_group:]
    si = pl.program_id(1)
    qr = qr_ref[0]
    rows = qr.shape[0]
    bias = bias_ref[...]
    tri = tri_ref[...]

    @pl.when(si == 0)
    def _():
        trow = lax.broadcasted_iota(jnp.int32, (rows, PAGE), 0) // n_heads
        s = lax.broadcasted_iota(jnp.int32, (rows, PAGE), 1)
        c, acc = _sb_block(qr, kn_ref[0], vn_ref[0], bias, tri, jnp.zeros((rows, PAGE), f32),
                           jnp.zeros(acc_ref.shape, f32), s < trow)
        c_ref[...] = c
        acc_ref[...] = acc

    c = c_ref[...]
    acc = acc_ref[...]
    for g in range(n_group):
        c, acc = _sb_block(qr, _bf(kp_refs[g][0, 0]), _bf(vp_refs[g][0, 0]), bias, tri, c, acc, None)
    c_ref[...] = c
    acc_ref[...] = acc

    @pl.when(si == pl.num_programs(1) - 1)
    def _():
        w = acc.shape[1]
        own = jnp.where(hm_ref[...] > 0, acc, 0.0).reshape(rows // n_heads, n_heads, w)
        o = jnp.sum(own, axis=1)
        o_ref[0] = o


def _attn_sample(q_rows, k_new, v_new, bias_rows, tri, head_mask, cache_k, cache_v, page_table, layer, *, n_group,
                 t_new, n_heads):
    db, rows, wb = q_rows.shape
    n_pages = page_table.shape[1]
    n_steps = n_pages // n_group

    def page_map(g):
        def f(b, s, pt):
            return (layer, pt[b, n_pages - 1 - (s * n_group + g)], 0, 0)
        return f

    page_specs = [pl.BlockSpec((1, 1, PAGE, wb), page_map(g)) for g in range(n_group)]
    seq = lambda b, s, pt: (b, 0, 0)
    const2 = lambda b, s, pt: (0, 0)
    t_rows = rows // n_heads
    kern = functools.partial(_attn_sample_kernel, n_group=n_group, t_new=t_new, n_heads=n_heads)
    gs = pltpu.PrefetchScalarGridSpec(
        num_scalar_prefetch=1,
        grid=(db, n_steps),
        in_specs=[
            pl.BlockSpec((1, rows, wb), seq),
            pl.BlockSpec((1, PAGE, wb), seq),
            pl.BlockSpec((1, PAGE, wb), seq),
            pl.BlockSpec(bias_rows.shape, const2),
            pl.BlockSpec(tri.shape, const2),
            pl.BlockSpec(head_mask.shape, const2),
        ] + page_specs + page_specs,
        out_specs=pl.BlockSpec((1, t_rows, wb), seq),
        scratch_shapes=[pltpu.VMEM((rows, wb), f32), pltpu.VMEM((rows, PAGE), f32)],
    )
    return pl.pallas_call(
        kern,
        grid_spec=gs,
        out_shape=jax.ShapeDtypeStruct((db, t_rows, wb), f32),
        compiler_params=_params(("arbitrary", "arbitrary")),
        name="attn_sample",
    )(page_table, q_rows, k_new, v_new, bias_rows, tri, head_mask, *([cache_k] * n_group), *([cache_v] * n_group))


def _proj_c_kernel(x_ref, g_ref, wc_ref, mu_ref, w0_ref, a0_ref, kk_ref, ka_ref, ww2_ref, wa2_ref, wg2_ref, bd_ref,
                   st_ref, r_ref, lw_ref, k_ref, v_ref, a_ref, b_ref, gg_ref, ns_ref, carry_ref,
                   *, bb, tt, wc, t_valid):
    ti = pl.program_id(1)

    @pl.when(ti == 0)
    def _():
        carry_ref[...] = st_ref[...]

    d = x_ref.shape[-1]
    cc = wc_ref.shape[1]
    x = x_ref[...].reshape(bb * tt, d)
    xn = _bf(_rms(x, g_ref[...]))
    pc = _dot(xn, wc_ref[...])
    tpos = lax.broadcasted_iota(jnp.int32, (bb, tt, cc), 1).reshape(bb * tt, cc)
    cprev = jnp.broadcast_to(carry_ref[...], (bb, tt, cc)).reshape(bb * tt, cc)
    prev = _shift_down(pc, 1, (cprev,), tpos)
    pcs = pc + (prev - pc) * mu_ref[...]
    r, kc, vc, lo = pcs[:, :wc], pcs[:, wc:2 * wc], pcs[:, 2 * wc:3 * wc], pcs[:, 3 * wc:]
    w_log = -jax.nn.softplus(-(w0_ref[...] + _dot(_bf(jnp.tanh(lo)), ww2_ref[...]))) - 0.5
    a = jax.nn.sigmoid(a0_ref[...] + _dot(_bf(lo), wa2_ref[...]))
    g = _dot(_bf(jax.nn.sigmoid(lo)), wg2_ref[...])
    kk = kc * kk_ref[...]
    kk = kk / jnp.maximum(jnp.sqrt(_segsum(kk * kk, bd_ref[...])), 1e-12)
    kmod = kc * (1.0 + (a - 1.0) * ka_ref[...])
    shp = (bb, tt, wc)
    r_ref[...] = r.reshape(shp)
    lw_ref[...] = (-jnp.exp(w_log)).reshape(shp)
    k_ref[...] = kmod.reshape(shp)
    v_ref[...] = vc.reshape(shp)
    a_ref[...] = (-kk).reshape(shp)
    b_ref[...] = (kk * a).reshape(shp)
    gg_ref[...] = g.reshape(shp)
    pc3 = pc.reshape(bb, tt, cc)
    carry_ref[...] = pc3[:, tt - 1:tt, :]

    last = t_valid - 1
    @pl.when(ti == last // tt)
    def _():
        ns_ref[...] = pc3[:, last % tt:last % tt + 1, :]


def _proj_c(x, g, w_c, mu, w0, a0, k_k, k_a, ww2, wa2, wg2, bd, state, *, t_valid, bb, tt):
    bg, tp, d = x.shape
    cc = w_c.shape[1]
    wc = w0.shape[1]
    grid = (bg // bb, tp // tt)
    blk = lambda b, t: (b, t, 0)
    kern = functools.partial(_proj_c_kernel, bb=bb, tt=tt, wc=wc, t_valid=t_valid)
    act = jax.ShapeDtypeStruct((bg, tp, wc), f32)
    return pl.pallas_call(
        kern,
        grid=grid,
        in_specs=[
            pl.BlockSpec((bb, tt, d), blk),
            _full((1, d)), _full(w_c.shape), _full((1, cc)), _full((1, wc)), _full((1, wc)), _full((1, wc)),
            _full((1, wc)), _full(ww2.shape), _full(wa2.shape), _full(wg2.shape), _full(bd.shape),
            pl.BlockSpec((bb, 1, cc), lambda b, t: (b, 0, 0)),
        ],
        out_specs=[pl.BlockSpec((bb, tt, wc), blk)] * 7 + [pl.BlockSpec((bb, 1, cc), lambda b, t: (b, 0, 0))],
        out_shape=[act] * 7 + [jax.ShapeDtypeStruct((bg, 1, cc), f32)],
        scratch_shapes=[pltpu.VMEM((bb, 1, cc), f32)],
        compiler_params=_params(("arbitrary", "arbitrary")),
        name="proj_c",
    )(x, g, w_c, mu, w0, a0, k_k, k_a, ww2, wa2, wg2, bd, state)


def _rwkv_kernel(r_ref, lw_ref, k_ref, v_ref, a_ref, b_ref, g_ref, s0_ref, rk_ref, lng_ref, lnb_ref, tril_ref,
                 bd_ref, y_ref, so_ref, s_ref, *, t_valid, ch, n_pairs):
    ci = pl.program_id(1)

    @pl.when(ci == 0)
    def _():
        s_ref[...] = s0_ref[0]

    n2 = 2 * ch
    tpos = lax.broadcasted_iota(jnp.int32, (ch, LANES), 0) + ci * ch
    valid = tpos < t_valid
    lane = lax.broadcasted_iota(jnp.int32, (ch, LANES), 1)
    first = lane < HEAD_DIM
    row = lax.broadcasted_iota(jnp.int32, (n2, n2), 0)
    col = lax.broadcasted_iota(jnp.int32, (n2, n2), 1)
    strict = row > col
    incl = row >= col
    eye = (row == col).astype(f32)
    tril = tril_ref[...]
    bd = bd_ref[...]

    def stack(x):
        return jnp.concatenate([jnp.where(first, x, 0.0), jnp.where(first, 0.0, x)], axis=0)

    for p in range(n_pairs):
        sl = slice(p * LANES, (p + 1) * LANES)
        lw = jnp.where(valid, lw_ref[0, :, sl], 0.0)
        h0, h1, h2 = _split3(lw)
        cum = _dot(tril, h0) + (_dot(tril, h1) + _dot(tril, h2))
        e_in = jnp.exp(cum)
        e_ex = jnp.exp(cum - lw)
        e_neg = jnp.exp(-cum)
        r = r_ref[0, :, sl]
        k = k_ref[0, :, sl]
        v = v_ref[0, :, sl]
        a2 = stack(a_ref[0, :, sl] * e_ex)
        r2 = stack(r * e_in)
        b2 = stack(jnp.where(valid, b_ref[0, :, sl], 0.0) * e_neg)
        k2 = stack(jnp.where(valid, k, 0.0) * e_neg)
        v2 = stack(v)
        l_ab = jnp.where(strict, _dot3(a2, b2, _dot_nt), 0.0)
        l_ak = jnp.where(strict, _dot3(a2, k2, _dot_nt), 0.0)
        m_rb = jnp.where(incl, _dot3(r2, b2, _dot_nt), 0.0)
        m_rk = jnp.where(incl, _dot3(r2, k2, _dot_nt), 0.0)
        tinv = eye + l_ab
        pw = l_ab
        steps = int(np.log2(ch)) - 1
        for _ in range(steps):
            pw = _dot3(pw, pw)
            tinv = tinv + _dot3(tinv, pw)
        s = s_ref[p]
        x = _dot3(a2, s, _dot_nt) + _dot3(l_ak, v2)
        u = _dot3(tinv, x)
        y2 = _dot3(r2, s, _dot_nt) + _dot3(m_rb, u) + _dot3(m_rk, v2)
        s_new = (s + _dot3(u, b2, _dot_tn) + _dot3(v2, k2, _dot_tn)) * e_in[ch - 1:ch, :]
        s_ref[p] = s_new
        y = y2[:ch] + y2[ch:]
        inv = 1.0 / HEAD_DIM
        mu = _segsum(y, bd) * inv
        dy = y - mu
        var = _segsum(dy * dy, bd) * inv
        yn = dy * lax.rsqrt(var + LNX_EPS) * lng_ref[:, sl] + lnb_ref[:, sl]
        bonus = _segsum(r * k * rk_ref[:, sl], bd) * v
        y_ref[0, :, sl] = (yn + bonus) * g_ref[0, :, sl]

    @pl.when(ci == pl.num_programs(1) - 1)
    def _():
        so_ref[0] = s_ref[...]


def _rwkv(r, lw, k, v, a, b, g, s0, r_k, lnx_g, lnx_b, tril, bd, *, t_valid):
    bg, tp, wc = r.shape
    ch = RWKV_CHUNK
    n_pairs = wc // LANES
    grid = (bg, tp // ch)
    blk = lambda b_, c: (b_, c, 0)
    sblk = lambda b_, c: (b_, 0, 0, 0)
    kern = functools.partial(_rwkv_kernel, t_valid=t_valid, ch=ch, n_pairs=n_pairs)
    act = pl.BlockSpec((1, ch, wc), blk)
    st = pl.BlockSpec((1, n_pairs, LANES, LANES), sblk)
    return pl.pallas_call(
        kern,
        grid=grid,
        in_specs=[act] * 7 + [st, _full((1, wc)), _full((1, wc)), _full((1, wc)), _full(tril.shape), _full(bd.shape)],
        out_specs=[act, st],
        out_shape=[jax.ShapeDtypeStruct((bg, tp, wc), f32), jax.ShapeDtypeStruct(s0.shape, f32)],
        scratch_shapes=[pltpu.VMEM((n_pairs, LANES, LANES), f32)],
        compiler_params=_params(("arbitrary", "arbitrary")),
        name="rwkv",
    )(r, lw, k, v, a, b, g, s0, r_k, lnx_g, lnx_b, tril, bd)


def _state_to_pairs(s):
    b, h, n, _ = s.shape
    s = s.reshape(b, h // 2, 2, n, n)
    eye2 = jnp.eye(2, dtype=s.dtype)
    bd = s[:, :, :, :, None, :] * eye2[None, None, :, None, :, None]
    return bd.reshape(b, h // 2, 2 * n, 2 * n)


def _pairs_to_state(sp, n):
    b, p = sp.shape[:2]
    s = sp.reshape(b, p, 2, n, 2, n)
    s = jnp.stack([s[:, :, 0, :, 0, :], s[:, :, 1, :, 1, :]], axis=2)
    return s.reshape(b, 2 * p, n, n)


def _merge_kernel(x_ref, ya_ref, o_ref, yc_ref, g_ref, wgb_ref, wgc_ref, wbb_ref, wbc_ref, wo_ref, g2_ref, wup_ref,
                  wdn_ref, out_ref):
    x = x_ref[...]
    xn = _bf(_rms(x, g_ref[...]))
    yb = _dot(_bf(o_ref[...]), wbb_ref[...])
    yc = _dot(_bf(yc_ref[...]), wbc_ref[...])
    mixed = ya_ref[...] + jax.nn.sigmoid(_dot(xn, wgb_ref[...])) * yb + jax.nn.sigmoid(_dot(xn, wgc_ref[...])) * yc
    x1 = x + _dot(_bf(mixed), wo_ref[...])
    xn2 = _bf(_rms(x1, g2_ref[...]))
    hid = jnp.square(jnp.maximum(_dot(xn2, wup_ref[...]), 0.0))
    out_ref[...] = x1 + _dot(_bf(hid), wdn_ref[...])


def _merge(x2, ya2, o2, yc2, g, w_gb, w_gc, w_bb, w_bc, w_o, g2, w_up, w_dn, *, tm):
    n, d = x2.shape
    wb = o2.shape[1]
    row = lambda i: (i, 0)
    ws = [w_gb, w_gc, w_bb, w_bc, w_o]
    return pl.pallas_call(
        _merge_kernel,
        grid=(n // tm,),
        in_specs=[pl.BlockSpec((tm, d), row), pl.BlockSpec((tm, d), row), pl.BlockSpec((tm, wb), row),
                  pl.BlockSpec((tm, wb), row), _full((1, d))] + [_full(w.shape) for w in ws]
                 + [_full((1, d)), _full(w_up.shape), _full(w_dn.shape)],
        out_specs=pl.BlockSpec((tm, d), row),
        out_shape=jax.ShapeDtypeStruct((n, d), f32),
        compiler_params=_params(("arbitrary",)),
        name="merge_mlp",
    )(x2, ya2, o2, yc2, g, *ws, g2, w_up, w_dn)


def _layer_weights(l, d, wa, wb, wc, cc, norm_mix_g, w_in, conv_w, q_norm_g, k_norm_g, mu_c, w0, w_w2, a0, w_a2, w_g2,
                   k_k, k_a, r_k, lnx_g, lnx_b, w_br_a, w_br_b, w_br_c, w_o, norm_mlp_g, w_up, w_down):
    s1 = 3 * wa
    s2 = s1 + 3 * wb
    s3 = s2 + cc
    win = w_in[l]
    n_h = wb // HEAD_DIM
    tail = cc - 3 * wc
    zpad = lambda w, before: jnp.pad(w, ((before, tail - before - w.shape[0]), (0, 0)))
    return dict(
        g_mix=norm_mix_g[l].reshape(1, d),
        w_a=_bf(win[:, :s1]), w_b=_bf(win[:, s1:s2]), w_c=_bf(win[:, s2:s3]),
        w_ga=_bf(win[:, s3:s3 + d]), w_gb=_bf(win[:, s3 + d:s3 + 2 * d]), w_gc=_bf(win[:, s3 + 2 * d:]),
        conv_w=conv_w[l],
        qg=jnp.tile(q_norm_g[l], n_h).reshape(1, wb), kg=jnp.tile(k_norm_g[l], n_h).reshape(1, wb),
        mu=mu_c[l].reshape(1, cc), w0=w0[l].reshape(1, wc), a0=a0[l].reshape(1, wc),
        k_k=k_k[l].reshape(1, wc), k_a=k_a[l].reshape(1, wc),
        ww2=_bf(zpad(w_w2[l], 0)), wa2=_bf(zpad(w_a2[l], LORA_W)), wg2=_bf(zpad(w_g2[l], LORA_W + LORA_A)),
        r_k=r_k[l].reshape(1, wc), lnx_g=lnx_g[l].reshape(1, wc), lnx_b=lnx_b[l].reshape(1, wc),
        w_br_a=_bf(w_br_a[l]), w_br_b=_bf(w_br_b[l]), w_br_c=_bf(w_br_c[l]), w_o=_bf(w_o[l]),
        g_mlp=norm_mlp_g[l].reshape(1, d), w_up=_bf(w_up[l]), w_dn=_bf(w_down[l]),
    )


def _pad_time(x, tp):
    return jnp.pad(x, ((0, 0), (0, tp - x.shape[1]), (0, 0)))


def _group_layer(x, w, consts, conv_state, shift_state, wkv_pairs, attn_fn, *, t_valid, bb, tt, tm):
    bg, tp, d = x.shape
    n = bg * tp
    wb = w["w_br_b"].shape[0]
    x2 = x.reshape(n, d)
    ya, conv_new = _mixer_a(x, w["g_mix"], w["w_a"], w["w_ga"], w["conv_w"], w["w_br_a"], conv_state,
                            t_valid=t_valid, bb=bb, tt=tt)
    q16, k16, v16, k32, v32 = _proj_b(x2, w["g_mix"], w["w_b"], w["qg"], w["kg"], consts["bd256"], tm=tm)
    sh3 = lambda t: t.reshape(bg, tp, wb)
    o = attn_fn(sh3(q16), sh3(k16), sh3(v16))
    r, lw, k, v, a, b, g, shift_new = _proj_c(x, w["g_mix"], w["w_c"], w["mu"], w["w0"], w["a0"], w["k_k"], w["k_a"],
                                              w["ww2"], w["wa2"], w["wg2"], consts["bd256"], shift_state,
                                              t_valid=t_valid, bb=bb, tt=tt)
    ch = RWKV_CHUNK
    tpr = -(-tp // ch) * ch
    ops = [r, lw, k, v, a, b, g]
    if tpr != tp:
        ops = [_pad_time(t, tpr) for t in ops]
    yc, wkv_new = _rwkv(*ops, wkv_pairs, w["r_k"], w["lnx_g"], w["lnx_b"], consts["tril_ch"], consts["bd128"],
                        t_valid=t_valid)
    yc = yc[:, :tp]
    x_new = _merge(x2, ya.reshape(n, d), o.reshape(n, wb), yc.reshape(n, wb), w["g_mix"], w["w_gb"], w["w_gc"],
                   w["w_br_b"], w["w_br_c"], w["w_o"], w["g_mlp"], w["w_up"], w["w_dn"], tm=tm)
    return x_new.reshape(bg, tp, d), conv_new, shift_new, wkv_new, sh3(k32), sh3(v32)


def kernel(x_prompt, x_sample, cache_k, cache_v, state_conv, state_shift, state_wkv, page_table, meta_tokens, norm_mix_g, w_in, conv_w, q_norm_g, k_norm_g, sb_bias, mu_c, w0, w_w2, a0, w_a2, w_g2, k_k, k_a, r_k, lnx_g, lnx_b, w_br_a, w_br_b, w_br_c, w_o, norm_mlp_g, w_up, w_down):
    bp, seq, d = x_prompt.shape
    db, dseq, _ = x_sample.shape
    depth = w_in.shape[0]
    wa = w_br_a.shape[1]
    wb = w_br_b.shape[1]
    wc = w_br_c.shape[1]
    cc = mu_c.shape[1]
    n_h = wb // HEAD_DIM
    n_hc = wc // HEAD_DIM
    n_pool, page = cache_k.shape[1], cache_k.shape[2]
    assert page == PAGE and cache_k.shape[3] * cache_k.shape[4] == wb
    n_pages = page_table.shape[1]

    tp_valid = seq + N_META
    tp_p = -(-tp_valid // LANES) * LANES
    tp_s = -(-dseq // 8) * 8
    meta = jnp.broadcast_to(meta_tokens.astype(x_prompt.dtype)[None], (bp, N_META, d))
    xp = _pad_time(jnp.concatenate([meta, x_prompt], axis=1), tp_p)
    xs = _pad_time(x_sample, tp_s)

    tq = LANES
    consts = dict(
        bd256=_block_diag_ones(2 * LANES, HEAD_DIM),
        bd128=_block_diag_ones(LANES, HEAD_DIM),
        tril_ch=jnp.asarray(np.tril(np.ones((RWKV_CHUNK, RWKV_CHUNK))), dtype=bf16),
        tri=jnp.asarray(np.tril(np.ones((tq, tq))), dtype=bf16),
    )
    ck = cache_k.reshape(depth, n_pool, PAGE, wb)
    cv = cache_v.reshape(depth, n_pool, PAGE, wb)
    rows = dseq * n_h
    ri = np.arange(rows)
    head_mask = jnp.asarray((ri[:, None] % n_h) == (np.arange(wb)[None, :] // HEAD_DIM), dtype=f32)
    n_group = _pick_tile(n_pages, 8, 1)

    tt_p = _pick_tile(tp_p, 528, 16)
    tm_p = _pick_tile(bp * tp_p, 512, 16)
    tm_s = _pick_tile(db * tp_s, 512, 8)

    zeros_conv = jnp.zeros((bp, CONV_W - 1, wa), f32)
    zeros_shift = jnp.zeros((bp, 1, cc), f32)
    zeros_wkv = jnp.zeros((bp, n_hc // 2, LANES, LANES), f32)

    outs = [[] for _ in range(10)]
    for l in range(depth):
        w = _layer_weights(l, d, wa, wb, wc, cc, norm_mix_g, w_in, conv_w, q_norm_g, k_norm_g, mu_c, w0, w_w2, a0,
                           w_a2, w_g2, k_k, k_a, r_k, lnx_g, lnx_b, w_br_a, w_br_b, w_br_c, w_o, norm_mlp_g, w_up,
                           w_down)
        bias = sb_bias[l].astype(f32)

        attn_p = lambda q, k, v: _attn_prompt(q, k, v, bias, consts["tri"], tq=tq)
        xp, cp, sp, wp, kp, vp = _group_layer(xp, w, consts, zeros_conv, zeros_shift, zeros_wkv, attn_p,
                                              t_valid=tp_valid, bb=1, tt=tt_p, tm=tm_p)

        def attn_s(q, k, v, l=l):
            q4 = q[:, :dseq].reshape(db, dseq, 1, n_h, HEAD_DIM)
            qr = (q4 * jnp.eye(n_h, dtype=q.dtype)[None, None, :, :, None]).reshape(db, rows, wb)
            pad_keys = lambda t: jnp.pad(t[:, :dseq], ((0, 0), (0, PAGE - dseq), (0, 0)))
            bias_rows = jnp.broadcast_to(jnp.tile(bias, dseq)[:, None], (rows, PAGE))
            o = _attn_sample(qr, pad_keys(k), pad_keys(v), bias_rows, consts["tri"], head_mask, ck, cv, page_table,
                             l, n_group=n_group, t_new=dseq, n_heads=n_h)
            return _pad_time(o, tp_s)

        xs, cs, ss, ws, ks, vs = _group_layer(xs, w, consts, state_conv[l], state_shift[l][:, None, :],
                                              _state_to_pairs(state_wkv[l]), attn_s,
                                              t_valid=dseq, bb=db, tt=tp_s, tm=tm_s)
        vals = (kp[:, :tp_valid].reshape(bp, tp_valid, n_h, HEAD_DIM), vp[:, :tp_valid].reshape(bp, tp_valid, n_h, HEAD_DIM),
                ks[:, :dseq].reshape(db, dseq, n_h, HEAD_DIM), vs[:, :dseq].reshape(db, dseq, n_h, HEAD_DIM),
                cp, cs, sp[:, 0], ss[:, 0], _pairs_to_state(wp, HEAD_DIM), _pairs_to_state(ws, HEAD_DIM))
        for lst, val in zip(outs, vals):
            lst.append(val)
    return (xp[:, N_META:tp_valid], xs[:, :dseq]) + tuple(jnp.stack(lst) for lst in outs)
```

```python
import functools

import numpy as np
import jax
import jax.numpy as jnp
from jax import lax
from jax.experimental import pallas as pl
from jax.experimental.pallas import tpu as pltpu

HEAD_DIM = 64
N_META = 16
CONV_W = 3
LORA_W = 64
LORA_A = 64
LORA_G = 128
NORM_EPS = 1e-6
LNX_EPS = 64e-5
LOG2E = 1.4426950408889634
LANES = 128
RWKV_CHUNK = 64
ATTN_KEYS = 128
ATTN_QUERIES = 384
PAGE = 128
VMEM_LIMIT = 56 * 1024 * 1024

f32 = jnp.float32
bf16 = jnp.bfloat16


def _bf(x):
    return x.astype(bf16)


def _dot(a, b):
    return jnp.dot(a, b, preferred_element_type=f32)


def _dot_nt(a, b):
    return lax.dot_general(a, b, (((1,), (1,)), ((), ())), preferred_element_type=f32)


def _dot_tn(a, b):
    return lax.dot_general(a, b, (((0,), (0,)), ((), ())), preferred_element_type=f32)


def _split2(x):
    hi = x.astype(bf16)
    lo = (x - hi.astype(f32)).astype(bf16)
    return hi, lo


def _split3(x):
    hi = x.astype(bf16)
    r1 = x - hi.astype(f32)
    mid = r1.astype(bf16)
    lo = (r1 - mid.astype(f32)).astype(bf16)
    return hi, mid, lo


def _dot_sel(x, sel):
    hi, lo = _split2(x)
    return _dot(hi, sel) + _dot(lo, sel)


def _rms(x, g):
    ms = jnp.mean(x * x, axis=-1, keepdims=True)
    return x * lax.rsqrt(ms + NORM_EPS) * g


def _segsum(x, bd):
    w = bd.shape[0]
    parts = [_dot_sel(x[:, s:s + w], bd) for s in range(0, x.shape[1], w)]
    return parts[0] if len(parts) == 1 else jnp.concatenate(parts, axis=1)


def _neg_abs(z):
    bits = lax.bitcast_convert_type(z, jnp.int32) | jnp.int32(-2147483648)
    return lax.bitcast_convert_type(bits, f32)


def _softplus2(z2):
    return jnp.maximum(z2, 0.0) + jnp.log(1.0 + jnp.exp2(_neg_abs(z2))) * LOG2E


def _pick_tile(n, target, mult):
    best = None
    for t in range(mult, min(n, target) + 1, mult):
        if n % t == 0:
            best = t
    if best is None:
        assert n <= target, (n, target, mult)
        best = n
    return best


def _block_diag_ones(n, blk):
    i = np.arange(n)
    return jnp.asarray((i[:, None] // blk) == (i[None, :] // blk), dtype=bf16)


def _full(shape):
    nd = len(shape)
    return pl.BlockSpec(shape, lambda *_: (0,) * nd)


def _params(sem):
    return pltpu.CompilerParams(dimension_semantics=sem, vmem_limit_bytes=VMEM_LIMIT)


def _shift_down(v, k, prev, tpos):
    out = pltpu.roll(v, k, 0)
    for j in range(k):
        out = jnp.where(tpos == j, prev[j], out)
    return out


def _mixer_a_kernel(x_ref, g_ref, wa_ref, wg_ref, cw_ref, wbr_ref, st_ref, y_ref, ns_ref, carry_ref,
                    *, bb, tt, wa, t_valid):
    ti = pl.program_id(1)

    @pl.when(ti == 0)
    def _():
        carry_ref[...] = st_ref[...]

    d = x_ref.shape[-1]
    x = x_ref[...].reshape(bb * tt, d)
    xn = _bf(_rms(x, g_ref[...]))
    pa = _dot(xn, wa_ref[...])
    gb, gc, h = pa[:, :wa], pa[:, wa:2 * wa], pa[:, 2 * wa:]
    u = gc * h
    tpos = lax.broadcasted_iota(jnp.int32, (bb, tt, wa), 1).reshape(bb * tt, wa)
    carry = carry_ref[...]
    c0 = jnp.broadcast_to(carry[:, 0:1, :], (bb, tt, wa)).reshape(bb * tt, wa)
    c1 = jnp.broadcast_to(carry[:, 1:2, :], (bb, tt, wa)).reshape(bb * tt, wa)
    u1 = _shift_down(u, 1, (c1,), tpos)
    u2 = _shift_down(u, 2, (c0, c1), tpos)
    cw = cw_ref[...]
    conv = cw[0:1] * u2 + cw[1:2] * u1 + cw[2:3] * u
    ya = _dot(_bf(gb * conv), wbr_ref[...])
    gate = jax.nn.sigmoid(_dot(xn, wg_ref[...]))
    y_ref[...] = (gate * ya).reshape(bb, tt, d)
    u3 = u.reshape(bb, tt, wa)
    carry_ref[...] = u3[:, tt - 2:tt, :]

    lo = t_valid - 2
    @pl.when(ti == lo // tt)
    def _():
        ns_ref[...] = u3[:, lo % tt:lo % tt + 2, :]


def _mixer_a(x, g, w_a, w_ga, conv_w, w_br, state, *, t_valid, bb, tt):
    bg, tp, d = x.shape
    wa = w_br.shape[0]
    assert (t_valid - 2) // tt == (t_valid - 1) // tt
    grid = (bg // bb, tp // tt)
    kern = functools.partial(_mixer_a_kernel, bb=bb, tt=tt, wa=wa, t_valid=t_valid)
    return pl.pallas_call(
        kern,
        grid=grid,
        in_specs=[
            pl.BlockSpec((bb, tt, d), lambda b, t: (b, t, 0)),
            _full((1, d)), _full(w_a.shape), _full(w_ga.shape), _full(conv_w.shape), _full(w_br.shape),
            pl.BlockSpec((bb, 2, wa), lambda b, t: (b, 0, 0)),
        ],
        out_specs=[
            pl.BlockSpec((bb, tt, d), lambda b, t: (b, t, 0)),
            pl.BlockSpec((bb, 2, wa), lambda b, t: (b, 0, 0)),
        ],
        out_shape=[jax.ShapeDtypeStruct((bg, tp, d), f32), jax.ShapeDtypeStruct((bg, 2, wa), f32)],
        scratch_shapes=[pltpu.VMEM((bb, 2, wa), f32)],
        compiler_params=_params(("arbitrary", "arbitrary")),
        name="mixer_a",
    )(x, g, w_a, w_ga, conv_w, w_br, state)


def _proj_b_kernel(x_ref, g_ref, wb_ref, qg_ref, kg_ref, bd_ref, q16_ref, k16_ref, v16_ref, k_ref, v_ref, *, wb):
    xn = _bf(_rms(x_ref[...], g_ref[...]))
    pb = _dot(xn, wb_ref[...])
    q, k, v = pb[:, :wb], pb[:, wb:2 * wb], pb[:, 2 * wb:]
    bd = bd_ref[...]
    inv = 1.0 / HEAD_DIM
    qn = q * lax.rsqrt(_segsum(q * q, bd) * inv + NORM_EPS) * qg_ref[...]
    kn = k * lax.rsqrt(_segsum(k * k, bd) * inv + NORM_EPS) * kg_ref[...]
    q16_ref[...] = _bf(qn * (LOG2E * HEAD_DIM ** -0.5))
    k16_ref[...] = _bf(kn)
    v16_ref[...] = _bf(v)
    k_ref[...] = kn
    v_ref[...] = v


def _proj_b(x2, g, w_b, qg, kg, bd, *, tm):
    n, d = x2.shape
    wb = w_b.shape[1] // 3
    row = lambda i: (i, 0)
    kern = functools.partial(_proj_b_kernel, wb=wb)
    return pl.pallas_call(
        kern,
        grid=(n // tm,),
        in_specs=[pl.BlockSpec((tm, d), row), _full((1, d)), _full(w_b.shape), _full((1, wb)), _full((1, wb)),
                  _full(bd.shape)],
        out_specs=[pl.BlockSpec((tm, wb), row)] * 5,
        out_shape=[jax.ShapeDtypeStruct((n, wb), bf16)] * 3 + [jax.ShapeDtypeStruct((n, wb), f32)] * 2,
        compiler_params=_params(("arbitrary",)),
        name="proj_b",
    )(x2, g, w_b, qg, kg, bd)


def _attn_prompt_kernel(bias_ref, q_ref, k_ref, v_ref, tri_ref, o_ref, *, tq, ks):
    qi = pl.program_id(2)
    q2 = q_ref[0]
    bias = bias_ref[0]
    tri2 = tri_ref[...]
    n_sub = tq // ks
    first = lax.broadcasted_iota(jnp.int32, (ks, LANES), 1) < HEAD_DIM
    row = lax.broadcasted_iota(jnp.int32, (tq, 2 * ks), 0)
    key = lax.broadcasted_iota(jnp.int32, (tq, 2 * ks), 1) & (ks - 1)

    def stack_heads(x):
        zero = jnp.zeros_like(x)
        return jnp.concatenate([jnp.where(first, x, zero), jnp.where(first, zero, x)], axis=0)

    def sub_block(start, c, acc, mask):
        k_st = stack_heads(k_ref[0, pl.ds(start, ks), :])
        v_st = stack_heads(v_ref[0, pl.ds(start, ks), :])
        z = _dot_nt(q2, k_st) + bias
        nl = _softplus2(z)
        if mask is not None:
            nl = jnp.where(mask, nl, 0.0)
        incl = _dot(_bf(nl), tri2)
        p = jnp.exp2(z - incl - c)
        if mask is not None:
            p = jnp.where(mask, p, 0.0)
        acc = acc + _dot(_bf(p), v_st)
        tot = jnp.concatenate([jnp.broadcast_to(incl[:, 0:1], (tq, ks)),
                               jnp.broadcast_to(incl[:, ks:ks + 1], (tq, ks))], axis=1)
        return c + tot, acc

    c = jnp.zeros((tq, 2 * ks), f32)
    acc = jnp.zeros((tq, LANES), f32)
    base = qi * tq
    for j in reversed(range(n_sub)):
        c, acc = sub_block(pl.multiple_of(base + j * ks, ks), c, acc, (key + j * ks) < row)

    def body(n, carry):
        c, acc = carry
        sb = (qi - 1 - n) * tq
        for j in reversed(range(n_sub)):
            c, acc = sub_block(pl.multiple_of(sb + j * ks, ks), c, acc, None)
        return c, acc

    c, acc = lax.fori_loop(0, qi, body, (c, acc))
    o_ref[0] = acc


def _attn_prompt(q16, k16, v16, bias_pairs, tri2, *, tq):
    bg, tp, wb = q16.shape
    ks = tri2.shape[0] // 2
    grid = (bg, wb // LANES, tp // tq)
    kern = functools.partial(_attn_prompt_kernel, tq=tq, ks=ks)
    return pl.pallas_call(
        kern,
        grid=grid,
        in_specs=[
            pl.BlockSpec((1, 1, 2 * ks), lambda b, h, i: (h, 0, 0)),
            pl.BlockSpec((1, tq, LANES), lambda b, h, i: (b, i, h)),
            pl.BlockSpec((1, tp, LANES), lambda b, h, i: (b, 0, h)),
            pl.BlockSpec((1, tp, LANES), lambda b, h, i: (b, 0, h)),
            _full(tri2.shape),
        ],
        out_specs=pl.BlockSpec((1, tq, LANES), lambda b, h, i: (b, i, h)),
        out_shape=jax.ShapeDtypeStruct((bg, tp, wb), f32),
        compiler_params=_params(("arbitrary", "arbitrary", "arbitrary")),
        name="attn_prompt",
    )(bias_pairs, q16, k16, v16, tri2)


def _scan_lanes(x):
    n = x.shape[1]
    lane = lax.broadcasted_iota(jnp.int32, x.shape, 1)
    d = 1
    while d < n:
        x = x + jnp.where(lane < n - d, pltpu.roll(x, n - d, 1), 0.0)
        d *= 2
    return x


def _attn_sample_kernel(pt_ref, q_ref, kn_ref, vn_ref, bias_ref, hm_ref, *rest, n_group, n_heads):
    kp_refs = rest[:n_group]
    vp_refs = rest[n_group:2 * n_group]
    o_ref, acc_ref, c_ref = rest[2 * n_group:]
    si = pl.program_id(1)
    q = q_ref[0]
    rows, w = q.shape
    bias = bias_ref[...]

    def block(kt, vt, c, acc, mask):
        reps = kt.shape[1] // LANES
        z = _dot(q, kt) + jnp.concatenate([bias] * reps, axis=1)
        nl = _softplus2(z)
        if mask is not None:
            nl = jnp.where(mask, nl, 0.0)
        incl = _scan_lanes(nl)
        p = jnp.exp2(z - incl - jnp.concatenate([c] * reps, axis=1))
        if mask is not None:
            p = jnp.where(mask, p, 0.0)
        return c + incl[:, 0:1], acc + _dot_nt(_bf(p), vt)

    @pl.when(si == 0)
    def _():
        trow = lax.broadcasted_iota(jnp.int32, (rows, LANES), 0) // n_heads
        pos = lax.broadcasted_iota(jnp.int32, (rows, LANES), 1)
        c, acc = block(kn_ref[0], vn_ref[0], jnp.zeros(c_ref.shape, f32), jnp.zeros(acc_ref.shape, f32), pos < trow)
        c_ref[...] = c
        acc_ref[...] = acc

    def pages(refs):
        return jnp.concatenate([_bf(r[0, 0].reshape(w, PAGE)) for r in reversed(refs)], axis=1)

    c, acc = block(pages(kp_refs), pages(vp_refs), c_ref[...], acc_ref[...], None)
    c_ref[...] = c
    acc_ref[...] = acc

    @pl.when(si == pl.num_programs(1) - 1)
    def _():
        own = jnp.where(hm_ref[...] > 0, acc, 0.0).reshape(rows // n_heads, n_heads, w)
        o_ref[0] = jnp.sum(own, axis=1)


def _attn_sample(q_rows, k_new, v_new, bias_rows, head_mask, cache_k, cache_v, page_table, layer, *, n_group,
                 n_heads):
    db, rows, wb = q_rows.shape
    n_pages = page_table.shape[1]
    n_steps = n_pages // n_group
    page_blk = (1, 1) + cache_k.shape[2:]

    def page_map(g):
        def f(b, s, pt):
            return (layer, pt[b, n_pages - 1 - (s * n_group + g)], 0, 0, 0)
        return f

    page_specs = [pl.BlockSpec(page_blk, page_map(g)) for g in range(n_group)]
    seq = lambda b, s, pt: (b, 0, 0)
    const2 = lambda b, s, pt: (0, 0)
    t_rows = rows // n_heads
    kern = functools.partial(_attn_sample_kernel, n_group=n_group, n_heads=n_heads)
    gs = pltpu.PrefetchScalarGridSpec(
        num_scalar_prefetch=1,
        grid=(db, n_steps),
        in_specs=[
            pl.BlockSpec((1, rows, wb), seq),
            pl.BlockSpec((1,) + k_new.shape[1:], seq),
            pl.BlockSpec((1,) + v_new.shape[1:], seq),
            pl.BlockSpec(bias_rows.shape, const2),
            pl.BlockSpec(head_mask.shape, const2),
        ] + page_specs + page_specs,
        out_specs=pl.BlockSpec((1, t_rows, wb), seq),
        scratch_shapes=[pltpu.VMEM((rows, wb), f32), pltpu.VMEM((rows, LANES), f32)],
    )
    return pl.pallas_call(
        kern,
        grid_spec=gs,
        out_shape=jax.ShapeDtypeStruct((db, t_rows, wb), f32),
        compiler_params=_params(("arbitrary", "arbitrary")),
        name="attn_sample",
    )(page_table, q_rows, k_new, v_new, bias_rows, head_mask, *([cache_k] * n_group), *([cache_v] * n_group))


def _proj_c_kernel(x_ref, g_ref, wc_ref, mu_ref, w0_ref, a0_ref, kk_ref, ka_ref, ww2_ref, wa2_ref, wg2_ref, bd_ref,
                   st_ref, r_ref, lw_ref, k_ref, v_ref, a_ref, b_ref, gg_ref, ns_ref, carry_ref,
                   *, bb, tt, wc, t_valid):
    ti = pl.program_id(1)

    @pl.when(ti == 0)
    def _():
        carry_ref[...] = st_ref[...]

    d = x_ref.shape[-1]
    cc = wc_ref.shape[1]
    x = x_ref[...].reshape(bb * tt, d)
    xn = _bf(_rms(x, g_ref[...]))
    pc = _dot(xn, wc_ref[...])
    tpos = lax.broadcasted_iota(jnp.int32, (bb, tt, cc), 1).reshape(bb * tt, cc)
    cprev = jnp.broadcast_to(carry_ref[...], (bb, tt, cc)).reshape(bb * tt, cc)
    prev = _shift_down(pc, 1, (cprev,), tpos)
    pcs = pc + (prev - pc) * mu_ref[...]
    r, kc, vc, lo = pcs[:, :wc], pcs[:, wc:2 * wc], pcs[:, 2 * wc:3 * wc], pcs[:, 3 * wc:]
    w_log = -jax.nn.softplus(-(w0_ref[...] + _dot(_bf(jnp.tanh(lo)), ww2_ref[...]))) - 0.5
    a = jax.nn.sigmoid(a0_ref[...] + _dot(_bf(lo), wa2_ref[...]))
    g = _dot(_bf(jax.nn.sigmoid(lo)), wg2_ref[...])
    kk = kc * kk_ref[...]
    kk = kk / jnp.maximum(jnp.sqrt(_segsum(kk * kk, bd_ref[...])), 1e-12)
    kmod = kc * (1.0 + (a - 1.0) * ka_ref[...])
    shp = (bb, tt, wc)
    r_ref[...] = r.reshape(shp)
    lw_ref[...] = (-jnp.exp(w_log)).reshape(shp)
    k_ref[...] = kmod.reshape(shp)
    v_ref[...] = vc.reshape(shp)
    a_ref[...] = (-kk).reshape(shp)
    b_ref[...] = (kk * a).reshape(shp)
    gg_ref[...] = g.reshape(shp)
    pc3 = pc.reshape(bb, tt, cc)
    carry_ref[...] = pc3[:, tt - 1:tt, :]

    last = t_valid - 1
    @pl.when(ti == last // tt)
    def _():
        ns_ref[...] = pc3[:, last % tt:last % tt + 1, :]


def _proj_c(x, g, w_c, mu, w0, a0, k_k, k_a, ww2, wa2, wg2, bd, state, *, t_valid, bb, tt):
    bg, tp, d = x.shape
    cc = w_c.shape[1]
    wc = w0.shape[1]
    grid = (bg // bb, tp // tt)
    blk = lambda b, t: (b, t, 0)
    kern = functools.partial(_proj_c_kernel, bb=bb, tt=tt, wc=wc, t_valid=t_valid)
    act = jax.ShapeDtypeStruct((bg, tp, wc), f32)
    return pl.pallas_call(
        kern,
        grid=grid,
        in_specs=[
            pl.BlockSpec((bb, tt, d), blk),
            _full((1, d)), _full(w_c.shape), _full((1, cc)), _full((1, wc)), _full((1, wc)), _full((1, wc)),
            _full((1, wc)), _full(ww2.shape), _full(wa2.shape), _full(wg2.shape), _full(bd.shape),
            pl.BlockSpec((bb, 1, cc), lambda b, t: (b, 0, 0)),
        ],
        out_specs=[pl.BlockSpec((bb, tt, wc), blk)] * 7 + [pl.BlockSpec((bb, 1, cc), lambda b, t: (b, 0, 0))],
        out_shape=[act] * 7 + [jax.ShapeDtypeStruct((bg, 1, cc), f32)],
        scratch_shapes=[pltpu.VMEM((bb, 1, cc), f32)],
        compiler_params=_params(("arbitrary", "arbitrary")),
        name="proj_c",
    )(x, g, w_c, mu, w0, a0, k_k, k_a, ww2, wa2, wg2, bd, state)


def _rwkv_kernel(r_ref, lw_ref, k_ref, v_ref, a_ref, b_ref, g_ref, s0_ref, rk_ref, lng_ref, lnb_ref, tril3_ref,
                 bd_ref, y_ref, so_ref, s_ref, *, t_valid, ch, n_pairs):
    ci = pl.program_id(1)

    @pl.when(ci == 0)
    def _():
        s_ref[...] = s0_ref[0]

    n2 = 2 * ch
    tpos = lax.broadcasted_iota(jnp.int32, (ch, LANES), 0) + ci * ch
    valid = tpos < t_valid
    first = lax.broadcasted_iota(jnp.int32, (ch, LANES), 1) < HEAD_DIM
    row = lax.broadcasted_iota(jnp.int32, (n2, n2), 0)
    col = lax.broadcasted_iota(jnp.int32, (n2, n2), 1)
    strict = row > col
    incl = row >= col
    eye = (row == col).astype(f32)
    tril3 = tril3_ref[...]
    bd = bd_ref[...]
    zb = jnp.zeros((ch, LANES), bf16)

    def stack(x):
        xb = _bf(x)
        return jnp.concatenate([jnp.where(first, xb, zb), jnp.where(first, zb, xb)], axis=0)

    pairs = range(n_pairs)
    sls = [slice(p * LANES, (p + 1) * LANES) for p in pairs]
    lw = [jnp.where(valid, lw_ref[0, :, sl], 0.0) for sl in sls]
    cum = [_dot(tril3, jnp.concatenate(_split3(x), axis=0)) for x in lw]
    e_in = [jnp.exp(c) for c in cum]
    e_neg = [jnp.exp(-c) for c in cum]
    r = [r_ref[0, :, sl] for sl in sls]
    k = [k_ref[0, :, sl] for sl in sls]
    v = [v_ref[0, :, sl] for sl in sls]
    v2 = [stack(x) for x in v]
    ar = [jnp.concatenate([stack(a_ref[0, :, sl] * jnp.exp(c - x)), stack(rr * ei)], axis=0)
          for sl, c, x, rr, ei in zip(sls, cum, lw, r, e_in)]
    bk = [jnp.concatenate([stack(jnp.where(valid, b_ref[0, :, sl], 0.0) * en),
                           stack(jnp.where(valid, kk, 0.0) * en)], axis=0)
          for sl, kk, en in zip(sls, k, e_neg)]
    gm = [_dot_nt(x, y) for x, y in zip(ar, bk)]
    l_ab = [jnp.where(strict, g[:n2, :n2], 0.0) for g in gm]
    lm = [_bf(jnp.concatenate([jnp.where(strict, g[:n2, n2:], 0.0), jnp.where(incl, g[n2:, n2:], 0.0)], axis=0))
          for g in gm]
    m_rb = [_bf(jnp.where(incl, g[n2:, :n2], 0.0)) for g in gm]
    part_v = [_dot(x, y) for x, y in zip(lm, v2)]
    tk = [eye + x for x in l_ab]
    pk = [_dot(_bf(x), _bf(x)) for x in l_ab]
    levels = int(np.log2(ch)) - 1
    for i in range(levels):
        pb = [_bf(x) for x in pk]
        if i < levels - 1:
            both = [_dot(x, jnp.concatenate([x, _bf(t)], axis=1)) for x, t in zip(pb, tk)]
            pk = [x[:, :n2] for x in both]
            tk = [t + x[:, n2:] for t, x in zip(tk, both)]
        else:
            tk = [t + _dot(x, _bf(t)) for t, x in zip(tk, pb)]
    s = [s_ref[p] for p in pairs]
    part_s = [_dot_nt(x, _bf(y)) for x, y in zip(ar, s)]
    u = [_bf(_dot(_bf(t), _bf(ps[:n2] + pv[:n2]))) for t, ps, pv in zip(tk, part_s, part_v)]
    y2 = [ps[n2:] + pv[n2:] + _dot(m, uu) for ps, pv, m, uu in zip(part_s, part_v, m_rb, u)]
    s_new = [(ss + _dot_tn(jnp.concatenate([uu, vv], axis=0), y)) * ei[ch - 1:ch, :]
             for ss, uu, vv, y, ei in zip(s, u, v2, bk, e_in)]
    y = [x[:ch] + x[ch:] for x in y2]
    inv = 1.0 / HEAD_DIM
    mu = [_dot(_bf(x), bd) * inv for x in y]
    dy = [x - m for x, m in zip(y, mu)]
    var = [_dot(_bf(x * x), bd) * inv for x in dy]
    bonus = [_dot(_bf(rr * kk * rk_ref[:, sl]), bd) * vv for rr, kk, vv, sl in zip(r, k, v, sls)]
    outs = [(x * lax.rsqrt(vr + LNX_EPS) * lng_ref[:, sl] + lnb_ref[:, sl] + bo) * g_ref[0, :, sl]
            for x, vr, bo, sl in zip(dy, var, bonus, sls)]
    for p in pairs:
        s_ref[p] = s_new[p]
    y_ref[0] = jnp.concatenate(outs, axis=1)

    @pl.when(ci == pl.num_programs(1) - 1)
    def _():
        so_ref[0] = s_ref[...]


def _rwkv(r, lw, k, v, a, b, g, s0, r_k, lnx_g, lnx_b, tril3, bd, *, t_valid):
    bg, tp, wc = r.shape
    ch = RWKV_CHUNK
    n_pairs = wc // LANES
    grid = (bg, tp // ch)
    blk = lambda b_, c: (b_, c, 0)
    sblk = lambda b_, c: (b_, 0, 0, 0)
    kern = functools.partial(_rwkv_kernel, t_valid=t_valid, ch=ch, n_pairs=n_pairs)
    act = pl.BlockSpec((1, ch, wc), blk)
    st = pl.BlockSpec((1, n_pairs, LANES, LANES), sblk)
    return pl.pallas_call(
        kern,
        grid=grid,
        in_specs=[act] * 7 + [st, _full((1, wc)), _full((1, wc)), _full((1, wc)), _full(tril3.shape),
                              _full(bd.shape)],
        out_specs=[act, st],
        out_shape=[jax.ShapeDtypeStruct((bg, tp, wc), f32), jax.ShapeDtypeStruct(s0.shape, f32)],
        scratch_shapes=[pltpu.VMEM((n_pairs, LANES, LANES), f32)],
        compiler_params=_params(("arbitrary", "arbitrary")),
        name="rwkv",
    )(r, lw, k, v, a, b, g, s0, r_k, lnx_g, lnx_b, tril3, bd)


def _state_to_pairs(s):
    b, h, n, _ = s.shape
    s = s.reshape(b, h // 2, 2, n, n)
    eye2 = jnp.eye(2, dtype=s.dtype)
    bd = s[:, :, :, :, None, :] * eye2[None, None, :, None, :, None]
    return bd.reshape(b, h // 2, 2 * n, 2 * n)


def _pairs_to_state(sp, n):
    b, p = sp.shape[:2]
    s = sp.reshape(b, p, 2, n, 2, n)
    s = jnp.stack([s[:, :, 0, :, 0, :], s[:, :, 1, :, 1, :]], axis=2)
    return s.reshape(b, 2 * p, n, n)


def _merge_kernel(x_ref, ya_ref, o_ref, yc_ref, g_ref, wgb_ref, wgc_ref, wbb_ref, wbc_ref, wo_ref, g2_ref, wup_ref,
                  wdn_ref, out_ref):
    x = x_ref[...]
    xn = _bf(_rms(x, g_ref[...]))
    yb = _dot(_bf(o_ref[...]), wbb_ref[...])
    yc = _dot(_bf(yc_ref[...]), wbc_ref[...])
    mixed = ya_ref[...] + jax.nn.sigmoid(_dot(xn, wgb_ref[...])) * yb + jax.nn.sigmoid(_dot(xn, wgc_ref[...])) * yc
    x1 = x + _dot(_bf(mixed), wo_ref[...])
    xn2 = _bf(_rms(x1, g2_ref[...]))
    hid = jnp.square(jnp.maximum(_dot(xn2, wup_ref[...]), 0.0))
    out_ref[...] = x1 + _dot(_bf(hid), wdn_ref[...])


def _merge(x2, ya2, o2, yc2, g, w_gb, w_gc, w_bb, w_bc, w_o, g2, w_up, w_dn, *, tm):
    n, d = x2.shape
    wb = o2.shape[1]
    row = lambda i: (i, 0)
    ws = [w_gb, w_gc, w_bb, w_bc, w_o]
    return pl.pallas_call(
        _merge_kernel,
        grid=(n // tm,),
        in_specs=[pl.BlockSpec((tm, d), row), pl.BlockSpec((tm, d), row), pl.BlockSpec((tm, wb), row),
                  pl.BlockSpec((tm, wb), row), _full((1, d))] + [_full(w.shape) for w in ws]
                 + [_full((1, d)), _full(w_up.shape), _full(w_dn.shape)],
        out_specs=pl.BlockSpec((tm, d), row),
        out_shape=jax.ShapeDtypeStruct((n, d), f32),
        compiler_params=_params(("arbitrary",)),
        name="merge_mlp",
    )(x2, ya2, o2, yc2, g, *ws, g2, w_up, w_dn)


def _layer_weights(l, d, wa, wb, wc, cc, norm_mix_g, w_in, conv_w, q_norm_g, k_norm_g, mu_c, w0, w_w2, a0, w_a2, w_g2,
                   k_k, k_a, r_k, lnx_g, lnx_b, w_br_a, w_br_b, w_br_c, w_o, norm_mlp_g, w_up, w_down):
    s1 = 3 * wa
    s2 = s1 + 3 * wb
    s3 = s2 + cc
    win = w_in[l]
    n_h = wb // HEAD_DIM
    tail = cc - 3 * wc
    zpad = lambda w, before: jnp.pad(w, ((before, tail - before - w.shape[0]), (0, 0)))
    return dict(
        g_mix=norm_mix_g[l].reshape(1, d),
        w_a=_bf(win[:, :s1]), w_b=_bf(win[:, s1:s2]), w_c=_bf(win[:, s2:s3]),
        w_ga=_bf(win[:, s3:s3 + d]), w_gb=_bf(win[:, s3 + d:s3 + 2 * d]), w_gc=_bf(win[:, s3 + 2 * d:]),
        conv_w=conv_w[l],
        qg=jnp.tile(q_norm_g[l], n_h).reshape(1, wb), kg=jnp.tile(k_norm_g[l], n_h).reshape(1, wb),
        mu=mu_c[l].reshape(1, cc), w0=w0[l].reshape(1, wc), a0=a0[l].reshape(1, wc),
        k_k=k_k[l].reshape(1, wc), k_a=k_a[l].reshape(1, wc),
        ww2=_bf(zpad(w_w2[l], 0)), wa2=_bf(zpad(w_a2[l], LORA_W)), wg2=_bf(zpad(w_g2[l], LORA_W + LORA_A)),
        r_k=r_k[l].reshape(1, wc), lnx_g=lnx_g[l].reshape(1, wc), lnx_b=lnx_b[l].reshape(1, wc),
        w_br_a=_bf(w_br_a[l]), w_br_b=_bf(w_br_b[l]), w_br_c=_bf(w_br_c[l]), w_o=_bf(w_o[l]),
        g_mlp=norm_mlp_g[l].reshape(1, d), w_up=_bf(w_up[l]), w_dn=_bf(w_down[l]),
    )


def _pad_time(x, tp):
    return jnp.pad(x, ((0, 0), (0, tp - x.shape[1]), (0, 0)))


def _group_layer(x, w, consts, conv_state, shift_state, wkv_pairs, attn_fn, *, t_valid, bb, tt, tm):
    bg, tp, d = x.shape
    n = bg * tp
    wb = w["w_br_b"].shape[0]
    x2 = x.reshape(n, d)
    ya, conv_new = _mixer_a(x, w["g_mix"], w["w_a"], w["w_ga"], w["conv_w"], w["w_br_a"], conv_state,
                            t_valid=t_valid, bb=bb, tt=tt)
    q16, k16, v16, k32, v32 = _proj_b(x2, w["g_mix"], w["w_b"], w["qg"], w["kg"], consts["bd256"], tm=tm)
    sh3 = lambda t: t.reshape(bg, tp, wb)
    o = attn_fn(sh3(q16), sh3(k16), sh3(v16))
    r, lw, k, v, a, b, g, shift_new = _proj_c(x, w["g_mix"], w["w_c"], w["mu"], w["w0"], w["a0"], w["k_k"], w["k_a"],
                                              w["ww2"], w["wa2"], w["wg2"], consts["bd256"], shift_state,
                                              t_valid=t_valid, bb=bb, tt=tt)
    ch = RWKV_CHUNK
    tpr = -(-tp // ch) * ch
    ops = [r, lw, k, v, a, b, g]
    if tpr != tp:
        ops = [_pad_time(t, tpr) for t in ops]
    yc, wkv_new = _rwkv(*ops, wkv_pairs, w["r_k"], w["lnx_g"], w["lnx_b"], consts["tril3"], consts["bd128"],
                        t_valid=t_valid)
    yc = yc[:, :tp]
    x_new = _merge(x2, ya.reshape(n, d), o.reshape(n, wb), yc.reshape(n, wb), w["g_mix"], w["w_gb"], w["w_gc"],
                   w["w_br_b"], w["w_br_c"], w["w_o"], w["g_mlp"], w["w_up"], w["w_dn"], tm=tm)
    return x_new.reshape(bg, tp, d), conv_new, shift_new, wkv_new, sh3(k32), sh3(v32)


def kernel(x_prompt, x_sample, cache_k, cache_v, state_conv, state_shift, state_wkv, page_table, meta_tokens, norm_mix_g, w_in, conv_w, q_norm_g, k_norm_g, sb_bias, mu_c, w0, w_w2, a0, w_a2, w_g2, k_k, k_a, r_k, lnx_g, lnx_b, w_br_a, w_br_b, w_br_c, w_o, norm_mlp_g, w_up, w_down):
    bp, seq, d = x_prompt.shape
    db, dseq, _ = x_sample.shape
    depth = w_in.shape[0]
    wa = w_br_a.shape[1]
    wb = w_br_b.shape[1]
    wc = w_br_c.shape[1]
    cc = mu_c.shape[1]
    n_h = wb // HEAD_DIM
    n_hc = wc // HEAD_DIM
    n_pool, page = cache_k.shape[1], cache_k.shape[2]
    assert page == PAGE and cache_k.shape[3] == n_h and cache_k.shape[4] == HEAD_DIM
    assert n_h & (n_h - 1) == 0 and dseq * n_h <= LANES
    n_pages = page_table.shape[1]

    tp_valid = seq + N_META
    tp_p = -(-tp_valid // LANES) * LANES
    tp_s = -(-dseq // 8) * 8
    meta = jnp.broadcast_to(meta_tokens.astype(x_prompt.dtype)[None], (bp, N_META, d))
    xp = _pad_time(jnp.concatenate([meta, x_prompt], axis=1), tp_p)
    xs = _pad_time(x_sample, tp_s)

    ks = ATTN_KEYS
    tq = _pick_tile(tp_p, ATTN_QUERIES, ks)
    tri = np.tril(np.ones((ks, ks)))
    tril = np.tril(np.ones((RWKV_CHUNK, RWKV_CHUNK)))
    consts = dict(
        bd256=_block_diag_ones(2 * LANES, HEAD_DIM),
        bd128=_block_diag_ones(LANES, HEAD_DIM),
        tril3=jnp.asarray(np.concatenate([tril] * 3, axis=1), dtype=bf16),
        tri2=jnp.asarray(np.kron(np.eye(2), tri), dtype=bf16),
    )
    ck = jnp.transpose(cache_k, (0, 1, 3, 4, 2))
    cv = jnp.transpose(cache_v, (0, 1, 3, 4, 2))
    rows = dseq * n_h
    head_mask = jnp.asarray((np.arange(rows)[:, None] % n_h) == (np.arange(wb)[None, :] // HEAD_DIM), dtype=f32)
    n_group = _pick_tile(n_pages, 16, 1)

    tt_p = _pick_tile(tp_p, 528, 16)
    tm_p = _pick_tile(bp * tp_p, 512, 16)
    tm_s = _pick_tile(db * tp_s, 512, 8)

    zeros_conv = jnp.zeros((bp, CONV_W - 1, wa), f32)
    zeros_shift = jnp.zeros((bp, 1, cc), f32)
    zeros_wkv = jnp.zeros((bp, n_hc // 2, LANES, LANES), f32)

    outs = [[] for _ in range(10)]
    for l in range(depth):
        w = _layer_weights(l, d, wa, wb, wc, cc, norm_mix_g, w_in, conv_w, q_norm_g, k_norm_g, mu_c, w0, w_w2, a0,
                           w_a2, w_g2, k_k, k_a, r_k, lnx_g, lnx_b, w_br_a, w_br_b, w_br_c, w_o, norm_mlp_g, w_up,
                           w_down)
        bias2 = sb_bias[l].astype(f32) * LOG2E
        bias_pairs = jnp.repeat(bias2.reshape(n_h // 2, 1, 2), ks, axis=2)

        attn_p = lambda q, k, v: _attn_prompt(q, k, v, bias_pairs, consts["tri2"], tq=tq)
        xp, cp, sp, wp, kp, vp = _group_layer(xp, w, consts, zeros_conv, zeros_shift, zeros_wkv, attn_p,
                                              t_valid=tp_valid, bb=1, tt=tt_p, tm=tm_p)

        def attn_s(q, k, v, l=l):
            q4 = q[:, :dseq].reshape(db, dseq, 1, n_h, HEAD_DIM)
            qr = (q4 * jnp.eye(n_h, dtype=q.dtype)[None, None, :, :, None]).reshape(db, rows, wb)
            as_page = lambda t: jnp.pad(jnp.swapaxes(t[:, :dseq], 1, 2), ((0, 0), (0, 0), (0, LANES - dseq)))
            bias_rows = jnp.broadcast_to(jnp.tile(bias2, dseq)[:, None], (rows, LANES))
            o = _attn_sample(qr, as_page(k), as_page(v), bias_rows, head_mask, ck, cv, page_table, l,
                             n_group=n_group, n_heads=n_h)
            return _pad_time(o, tp_s)

        xs, cs, ss, ws, ks_new, vs_new = _group_layer(xs, w, consts, state_conv[l], state_shift[l][:, None, :],
                                                      _state_to_pairs(state_wkv[l]), attn_s,
                                                      t_valid=dseq, bb=db, tt=tp_s, tm=tm_s)
        heads = lambda t, n: t[:, :n].reshape(t.shape[0], n, n_h, HEAD_DIM)
        vals = (heads(kp, tp_valid), heads(vp, tp_valid), heads(ks_new, dseq), heads(vs_new, dseq),
                cp, cs, sp[:, 0], ss[:, 0], _pairs_to_state(wp, HEAD_DIM), _pairs_to_state(ws, HEAD_DIM))
        for lst, val in zip(outs, vals):
            lst.append(val)
    return (xp[:, N_META:tp_valid], xs[:, :dseq]) + tuple(jnp.stack(lst) for lst in outs)
```

```python
import functools

import numpy as np
import jax
import jax.numpy as jnp
from jax import lax
from jax.experimental import pallas as pl
from jax.experimental.pallas import tpu as pltpu

HEAD_DIM = 64
N_META = 16
CONV_W = 3
LORA_W = 64
LORA_A = 64
LORA_G = 128
NORM_EPS = 1e-6
LNX_EPS = 64e-5
LOG2E = 1.4426950408889634
LANES = 128
RWKV_CHUNK = 64
RWKV_SEQS = 2
ATTN_KEYS = 128
ATTN_QUERIES = 384
PAGE = 128
VMEM_LIMIT = 56 * 1024 * 1024

f32 = jnp.float32
bf16 = jnp.bfloat16


def _bf(x):
    return x.astype(bf16)


def _dot(a, b):
    return jnp.dot(a, b, preferred_element_type=f32)


def _dot_nt(a, b):
    return lax.dot_general(a, b, (((1,), (1,)), ((), ())), preferred_element_type=f32)


def _dot_tn(a, b):
    return lax.dot_general(a, b, (((0,), (0,)), ((), ())), preferred_element_type=f32)


def _split2(x):
    hi = x.astype(bf16)
    lo = (x - hi.astype(f32)).astype(bf16)
    return hi, lo


def _split3(x):
    hi = x.astype(bf16)
    r1 = x - hi.astype(f32)
    mid = r1.astype(bf16)
    lo = (r1 - mid.astype(f32)).astype(bf16)
    return hi, mid, lo


def _dot_sel(x, sel):
    hi, lo = _split2(x)
    return _dot(hi, sel) + _dot(lo, sel)


def _rms(x, g):
    ms = jnp.mean(x * x, axis=-1, keepdims=True)
    return x * lax.rsqrt(ms + NORM_EPS) * g


def _segsum(x, bd):
    w = bd.shape[0]
    parts = [_dot_sel(x[:, s:s + w], bd) for s in range(0, x.shape[1], w)]
    return parts[0] if len(parts) == 1 else jnp.concatenate(parts, axis=1)


def _neg_abs(z):
    bits = lax.bitcast_convert_type(z, jnp.int32) | jnp.int32(-2147483648)
    return lax.bitcast_convert_type(bits, f32)


def _softplus2(z2):
    return jnp.maximum(z2, 0.0) + jnp.log(1.0 + jnp.exp2(_neg_abs(z2))) * LOG2E


def _pick_tile(n, target, mult):
    best = None
    for t in range(mult, min(n, target) + 1, mult):
        if n % t == 0:
            best = t
    if best is None:
        assert n <= target, (n, target, mult)
        best = n
    return best


def _block_diag_ones(n, blk):
    i = np.arange(n)
    return jnp.asarray((i[:, None] // blk) == (i[None, :] // blk), dtype=bf16)


def _full(shape):
    nd = len(shape)
    return pl.BlockSpec(shape, lambda *_: (0,) * nd)


def _params(sem):
    return pltpu.CompilerParams(dimension_semantics=sem, vmem_limit_bytes=VMEM_LIMIT)


def _shift_down(v, k, prev, tpos):
    out = pltpu.roll(v, k, 0)
    for j in range(k):
        out = jnp.where(tpos == j, prev[j], out)
    return out


def _mixer_a_kernel(x_ref, g_ref, wa_ref, wg_ref, cw_ref, wbr_ref, st_ref, y_ref, ns_ref, carry_ref,
                    *, bb, tt, wa, t_valid):
    ti = pl.program_id(1)

    @pl.when(ti == 0)
    def _():
        carry_ref[...] = st_ref[...]

    d = x_ref.shape[-1]
    x = x_ref[...].reshape(bb * tt, d)
    xn = _bf(_rms(x, g_ref[...]))
    pa = _dot(xn, wa_ref[...])
    gb, gc, h = pa[:, :wa], pa[:, wa:2 * wa], pa[:, 2 * wa:]
    u = gc * h
    tpos = lax.broadcasted_iota(jnp.int32, (bb, tt, wa), 1).reshape(bb * tt, wa)
    carry = carry_ref[...]
    c0 = jnp.broadcast_to(carry[:, 0:1, :], (bb, tt, wa)).reshape(bb * tt, wa)
    c1 = jnp.broadcast_to(carry[:, 1:2, :], (bb, tt, wa)).reshape(bb * tt, wa)
    u1 = _shift_down(u, 1, (c1,), tpos)
    u2 = _shift_down(u, 2, (c0, c1), tpos)
    cw = cw_ref[...]
    conv = cw[0:1] * u2 + cw[1:2] * u1 + cw[2:3] * u
    ya = _dot(_bf(gb * conv), wbr_ref[...])
    gate = jax.nn.sigmoid(_dot(xn, wg_ref[...]))
    y_ref[...] = (gate * ya).reshape(bb, tt, d)
    u3 = u.reshape(bb, tt, wa)
    carry_ref[...] = u3[:, tt - 2:tt, :]

    lo = t_valid - 2
    @pl.when(ti == lo // tt)
    def _():
        ns_ref[...] = u3[:, lo % tt:lo % tt + 2, :]


def _mixer_a(x, g, w_a, w_ga, conv_w, w_br, state, *, t_valid, bb, tt):
    bg, tp, d = x.shape
    wa = w_br.shape[0]
    assert (t_valid - 2) // tt == (t_valid - 1) // tt
    grid = (bg // bb, tp // tt)
    kern = functools.partial(_mixer_a_kernel, bb=bb, tt=tt, wa=wa, t_valid=t_valid)
    return pl.pallas_call(
        kern,
        grid=grid,
        in_specs=[
            pl.BlockSpec((bb, tt, d), lambda b, t: (b, t, 0)),
            _full((1, d)), _full(w_a.shape), _full(w_ga.shape), _full(conv_w.shape), _full(w_br.shape),
            pl.BlockSpec((bb, 2, wa), lambda b, t: (b, 0, 0)),
        ],
        out_specs=[
            pl.BlockSpec((bb, tt, d), lambda b, t: (b, t, 0)),
            pl.BlockSpec((bb, 2, wa), lambda b, t: (b, 0, 0)),
        ],
        out_shape=[jax.ShapeDtypeStruct((bg, tp, d), f32), jax.ShapeDtypeStruct((bg, 2, wa), f32)],
        scratch_shapes=[pltpu.VMEM((bb, 2, wa), f32)],
        compiler_params=_params(("arbitrary", "arbitrary")),
        name="mixer_a",
    )(x, g, w_a, w_ga, conv_w, w_br, state)


def _proj_b_kernel(x_ref, g_ref, wb_ref, qg_ref, kg_ref, bd_ref, q16_ref, k16_ref, v16_ref, k_ref, v_ref, *, wb):
    xn = _bf(_rms(x_ref[0], g_ref[...]))
    pb = _dot(xn, wb_ref[...])
    q, k, v = pb[:, :wb], pb[:, wb:2 * wb], pb[:, 2 * wb:]
    bd = bd_ref[...]
    inv = 1.0 / HEAD_DIM
    qn = q * lax.rsqrt(_segsum(q * q, bd) * inv + NORM_EPS) * qg_ref[...]
    kn = k * lax.rsqrt(_segsum(k * k, bd) * inv + NORM_EPS) * kg_ref[...]
    q16_ref[0] = _bf(qn * (LOG2E * HEAD_DIM ** -0.5))
    k16_ref[0] = _bf(kn)
    v16_ref[0] = _bf(v)
    k_ref[0] = kn
    v_ref[0] = v


def _proj_b(x, g, w_b, qg, kg, bd, *, tt, t_out):
    bg, tp, d = x.shape
    wb = w_b.shape[1] // 3
    blk = lambda b, t: (b, t, 0)
    kern = functools.partial(_proj_b_kernel, wb=wb)
    return pl.pallas_call(
        kern,
        grid=(bg, tp // tt),
        in_specs=[pl.BlockSpec((1, tt, d), blk), _full((1, d)), _full(w_b.shape), _full((1, wb)), _full((1, wb)),
                  _full(bd.shape)],
        out_specs=[pl.BlockSpec((1, tt, wb), blk)] * 5,
        out_shape=[jax.ShapeDtypeStruct((bg, tp, wb), bf16)] * 3 + [jax.ShapeDtypeStruct((bg, t_out, wb), f32)] * 2,
        compiler_params=_params(("arbitrary", "arbitrary")),
        name="proj_b",
    )(x, g, w_b, qg, kg, bd)


def _attn_prompt_kernel(bias_ref, q_ref, k_ref, v_ref, tri_ref, o_ref, *, tq, ks):
    qi = pl.program_id(2)
    q2 = q_ref[0]
    bias = bias_ref[0]
    tri2 = tri_ref[...]
    n_sub = tq // ks
    first = lax.broadcasted_iota(jnp.int32, (ks, LANES), 1) < HEAD_DIM
    row = lax.broadcasted_iota(jnp.int32, (tq, 2 * ks), 0)
    key = lax.broadcasted_iota(jnp.int32, (tq, 2 * ks), 1) & (ks - 1)

    def stack_heads(x):
        zero = jnp.zeros_like(x)
        return jnp.concatenate([jnp.where(first, x, zero), jnp.where(first, zero, x)], axis=0)

    def sub_block(start, c, acc, mask):
        k_st = stack_heads(k_ref[0, pl.ds(start, ks), :])
        v_st = stack_heads(v_ref[0, pl.ds(start, ks), :])
        z = _dot_nt(q2, k_st) + bias
        nl = _softplus2(z)
        if mask is not None:
            nl = jnp.where(mask, nl, 0.0)
        incl = _dot(_bf(nl), tri2)
        p = jnp.exp2(z - incl - c)
        if mask is not None:
            p = jnp.where(mask, p, 0.0)
        acc = acc + _dot(_bf(p), v_st)
        tot = jnp.concatenate([jnp.broadcast_to(incl[:, 0:1], (tq, ks)),
                               jnp.broadcast_to(incl[:, ks:ks + 1], (tq, ks))], axis=1)
        return c + tot, acc

    c = jnp.zeros((tq, 2 * ks), f32)
    acc = jnp.zeros((tq, LANES), f32)
    base = qi * tq
    for j in reversed(range(n_sub)):
        c, acc = sub_block(pl.multiple_of(base + j * ks, ks), c, acc, (key + j * ks) < row)

    def body(n, carry):
        c, acc = carry
        sb = (qi - 1 - n) * tq
        for j in reversed(range(n_sub)):
            c, acc = sub_block(pl.multiple_of(sb + j * ks, ks), c, acc, None)
        return c, acc

    c, acc = lax.fori_loop(0, qi, body, (c, acc))
    o_ref[0] = acc


def _attn_prompt(q16, k16, v16, bias_pairs, tri2, *, tq):
    bg, tp, wb = q16.shape
    ks = tri2.shape[0] // 2
    grid = (bg, wb // LANES, tp // tq)
    kern = functools.partial(_attn_prompt_kernel, tq=tq, ks=ks)
    return pl.pallas_call(
        kern,
        grid=grid,
        in_specs=[
            pl.BlockSpec((1, 1, 2 * ks), lambda b, h, i: (h, 0, 0)),
            pl.BlockSpec((1, tq, LANES), lambda b, h, i: (b, i, h)),
            pl.BlockSpec((1, tp, LANES), lambda b, h, i: (b, 0, h)),
            pl.BlockSpec((1, tp, LANES), lambda b, h, i: (b, 0, h)),
            _full(tri2.shape),
        ],
        out_specs=pl.BlockSpec((1, tq, LANES), lambda b, h, i: (b, i, h)),
        out_shape=jax.ShapeDtypeStruct((bg, tp, wb), f32),
        compiler_params=_params(("arbitrary", "arbitrary", "arbitrary")),
        name="attn_prompt",
    )(bias_pairs, q16, k16, v16, tri2)


def _scan_lanes(x):
    n = x.shape[1]
    lane = lax.broadcasted_iota(jnp.int32, x.shape, 1)
    d = 1
    while d < n:
        x = x + jnp.where(lane < n - d, pltpu.roll(x, n - d, 1), 0.0)
        d *= 2
    return x


def _attn_sample_kernel(pt_ref, q_ref, kn_ref, vn_ref, bias_ref, hm_ref, *rest, n_group, n_heads):
    kp_refs = rest[:n_group]
    vp_refs = rest[n_group:2 * n_group]
    o_ref, acc_ref, c_ref = rest[2 * n_group:]
    si = pl.program_id(1)
    q = q_ref[0]
    rows, w = q.shape
    bias = bias_ref[...]

    def block(kt, vt, c, acc, mask):
        reps = kt.shape[1] // LANES
        z = _dot(q, kt) + jnp.concatenate([bias] * reps, axis=1)
        nl = _softplus2(z)
        if mask is not None:
            nl = jnp.where(mask, nl, 0.0)
        incl = _scan_lanes(nl)
        p = jnp.exp2(z - incl - jnp.concatenate([c] * reps, axis=1))
        if mask is not None:
            p = jnp.where(mask, p, 0.0)
        return c + incl[:, 0:1], acc + _dot_nt(_bf(p), vt)

    @pl.when(si == 0)
    def _():
        trow = lax.broadcasted_iota(jnp.int32, (rows, LANES), 0) // n_heads
        pos = lax.broadcasted_iota(jnp.int32, (rows, LANES), 1)
        c, acc = block(kn_ref[0], vn_ref[0], jnp.zeros(c_ref.shape, f32), jnp.zeros(acc_ref.shape, f32), pos < trow)
        c_ref[...] = c
        acc_ref[...] = acc

    def pages(refs):
        return jnp.concatenate([_bf(r[0, 0].reshape(w, PAGE)) for r in reversed(refs)], axis=1)

    c, acc = block(pages(kp_refs), pages(vp_refs), c_ref[...], acc_ref[...], None)
    c_ref[...] = c
    acc_ref[...] = acc

    @pl.when(si == pl.num_programs(1) - 1)
    def _():
        own = jnp.where(hm_ref[...] > 0, acc, 0.0).reshape(rows // n_heads, n_heads, w)
        o_ref[0] = jnp.sum(own, axis=1)


def _attn_sample(q_rows, k_new, v_new, bias_rows, head_mask, cache_k, cache_v, page_table, layer, *, n_group,
                 n_heads):
    db, rows, wb = q_rows.shape
    n_pages = page_table.shape[1]
    n_steps = n_pages // n_group
    page_blk = (1, 1) + cache_k.shape[2:]

    def page_map(g):
        def f(b, s, pt):
            return (layer, pt[b, n_pages - 1 - (s * n_group + g)], 0, 0, 0)
        return f

    page_specs = [pl.BlockSpec(page_blk, page_map(g)) for g in range(n_group)]
    seq = lambda b, s, pt: (b, 0, 0)
    const2 = lambda b, s, pt: (0, 0)
    t_rows = rows // n_heads
    kern = functools.partial(_attn_sample_kernel, n_group=n_group, n_heads=n_heads)
    gs = pltpu.PrefetchScalarGridSpec(
        num_scalar_prefetch=1,
        grid=(db, n_steps),
        in_specs=[
            pl.BlockSpec((1, rows, wb), seq),
            pl.BlockSpec((1,) + k_new.shape[1:], seq),
            pl.BlockSpec((1,) + v_new.shape[1:], seq),
            pl.BlockSpec(bias_rows.shape, const2),
            pl.BlockSpec(head_mask.shape, const2),
        ] + page_specs + page_specs,
        out_specs=pl.BlockSpec((1, t_rows, wb), seq),
        scratch_shapes=[pltpu.VMEM((rows, wb), f32), pltpu.VMEM((rows, LANES), f32)],
    )
    return pl.pallas_call(
        kern,
        grid_spec=gs,
        out_shape=jax.ShapeDtypeStruct((db, t_rows, wb), f32),
        compiler_params=_params(("arbitrary", "arbitrary")),
        name="attn_sample",
    )(page_table, q_rows, k_new, v_new, bias_rows, head_mask, *([cache_k] * n_group), *([cache_v] * n_group))


def _proj_c_kernel(x_ref, g_ref, wc_ref, mu_ref, w0_ref, a0_ref, kk_ref, ka_ref, ww2_ref, wa2_ref, wg2_ref, bd_ref,
                   st_ref, r_ref, lw_ref, k_ref, v_ref, a_ref, b_ref, gg_ref, ns_ref, carry_ref,
                   *, bb, tt, wc, t_valid):
    ti = pl.program_id(1)

    @pl.when(ti == 0)
    def _():
        carry_ref[...] = st_ref[...]

    d = x_ref.shape[-1]
    cc = wc_ref.shape[1]
    x = x_ref[...].reshape(bb * tt, d)
    xn = _bf(_rms(x, g_ref[...]))
    pc = _dot(xn, wc_ref[...])
    tpos = lax.broadcasted_iota(jnp.int32, (bb, tt, cc), 1).reshape(bb * tt, cc)
    cprev = jnp.broadcast_to(carry_ref[...], (bb, tt, cc)).reshape(bb * tt, cc)
    prev = _shift_down(pc, 1, (cprev,), tpos)
    pcs = pc + (prev - pc) * mu_ref[...]
    r, kc, vc, lo = pcs[:, :wc], pcs[:, wc:2 * wc], pcs[:, 2 * wc:3 * wc], pcs[:, 3 * wc:]
    w_log = -jax.nn.softplus(-(w0_ref[...] + _dot(_bf(jnp.tanh(lo)), ww2_ref[...]))) - 0.5
    a = jax.nn.sigmoid(a0_ref[...] + _dot(_bf(lo), wa2_ref[...]))
    g = _dot(_bf(jax.nn.sigmoid(lo)), wg2_ref[...])
    kk = kc * kk_ref[...]
    kk = kk / jnp.maximum(jnp.sqrt(_segsum(kk * kk, bd_ref[...])), 1e-12)
    kmod = kc * (1.0 + (a - 1.0) * ka_ref[...])
    shp = (bb, tt, wc)
    lw_ref[...] = (-jnp.exp(w_log)).reshape(shp)
    for ref, val in ((r_ref, r), (k_ref, kmod), (v_ref, vc), (a_ref, -kk), (b_ref, kk * a), (gg_ref, g)):
        ref[...] = val.astype(ref.dtype).reshape(shp)
    pc3 = pc.reshape(bb, tt, cc)
    carry_ref[...] = pc3[:, tt - 1:tt, :]

    last = t_valid - 1
    @pl.when(ti == last // tt)
    def _():
        ns_ref[...] = pc3[:, last % tt:last % tt + 1, :]


def _proj_c(x, g, w_c, mu, w0, a0, k_k, k_a, ww2, wa2, wg2, bd, state, *, t_valid, bb, tt):
    bg, tp, d = x.shape
    cc = w_c.shape[1]
    wc = w0.shape[1]
    grid = (bg // bb, tp // tt)
    blk = lambda b, t: (b, t, 0)
    kern = functools.partial(_proj_c_kernel, bb=bb, tt=tt, wc=wc, t_valid=t_valid)
    act = jax.ShapeDtypeStruct((bg, tp, wc), bf16 if tt % 16 == 0 else f32)
    lw_act = jax.ShapeDtypeStruct((bg, tp, wc), f32)
    return pl.pallas_call(
        kern,
        grid=grid,
        in_specs=[
            pl.BlockSpec((bb, tt, d), blk),
            _full((1, d)), _full(w_c.shape), _full((1, cc)), _full((1, wc)), _full((1, wc)), _full((1, wc)),
            _full((1, wc)), _full(ww2.shape), _full(wa2.shape), _full(wg2.shape), _full(bd.shape),
            pl.BlockSpec((bb, 1, cc), lambda b, t: (b, 0, 0)),
        ],
        out_specs=[pl.BlockSpec((bb, tt, wc), blk)] * 7 + [pl.BlockSpec((bb, 1, cc), lambda b, t: (b, 0, 0))],
        out_shape=[act, lw_act] + [act] * 5 + [jax.ShapeDtypeStruct((bg, 1, cc), f32)],
        scratch_shapes=[pltpu.VMEM((bb, 1, cc), f32)],
        compiler_params=_params(("arbitrary", "arbitrary")),
        name="proj_c",
    )(x, g, w_c, mu, w0, a0, k_k, k_a, ww2, wa2, wg2, bd, state)


def _rwkv_kernel(r_ref, lw_ref, k_ref, v_ref, a_ref, b_ref, g_ref, s0_ref, rk_ref, lng_ref, lnb_ref, tril3_ref,
                 bd_ref, y_ref, so_ref, s_ref, *, t_valid, ch, n_pairs, nb):
    ci = pl.program_id(1)

    @pl.when(ci == 0)
    def _():
        s_ref[...] = s0_ref[...]

    n2 = 2 * ch
    tpos = lax.broadcasted_iota(jnp.int32, (ch, LANES), 0) + ci * ch
    valid = tpos < t_valid
    first = lax.broadcasted_iota(jnp.int32, (ch, LANES), 1) < HEAD_DIM
    row = lax.broadcasted_iota(jnp.int32, (n2, n2), 0)
    col = lax.broadcasted_iota(jnp.int32, (n2, n2), 1)
    strict = row > col
    incl = row >= col
    eye = (row == col).astype(f32)
    tril3 = tril3_ref[...]
    bd = bd_ref[...]
    zb = jnp.zeros((ch, LANES), bf16)

    def stack(x):
        xb = _bf(x)
        return jnp.concatenate([jnp.where(first, xb, zb), jnp.where(first, zb, xb)], axis=0)

    chains = [(i, p) for i in range(nb) for p in range(n_pairs)]
    sls = [slice(p * LANES, (p + 1) * LANES) for _, p in chains]
    ld = lambda ref: [ref[i, :, sl].astype(f32) for (i, _), sl in zip(chains, sls)]
    lw = [jnp.where(valid, x, 0.0) for x in ld(lw_ref)]
    cum = [_dot(tril3, jnp.concatenate(_split3(x), axis=0)) for x in lw]
    e_in = [jnp.exp(c) for c in cum]
    e_neg = [jnp.exp(-c) for c in cum]
    r, k, v, a, b, g = ld(r_ref), ld(k_ref), ld(v_ref), ld(a_ref), ld(b_ref), ld(g_ref)
    v2 = [stack(x) for x in v]
    ar = [jnp.concatenate([stack(aa * jnp.exp(c - x)), stack(rr * ei)], axis=0)
          for aa, c, x, rr, ei in zip(a, cum, lw, r, e_in)]
    bk = [jnp.concatenate([stack(jnp.where(valid, bb, 0.0) * en), stack(jnp.where(valid, kk, 0.0) * en)], axis=0)
          for bb, kk, en in zip(b, k, e_neg)]
    gm = [_dot_nt(x, y) for x, y in zip(ar, bk)]
    l_ab = [jnp.where(strict, g[:n2, :n2], 0.0) for g in gm]
    lm = [_bf(jnp.concatenate([jnp.where(strict, g[:n2, n2:], 0.0), jnp.where(incl, g[n2:, n2:], 0.0)], axis=0))
          for g in gm]
    m_rb = [_bf(jnp.where(incl, g[n2:, :n2], 0.0)) for g in gm]
    part_v = [_dot(x, y) for x, y in zip(lm, v2)]
    tk = [eye + x for x in l_ab]
    pk = [_dot(_bf(x), _bf(x)) for x in l_ab]
    levels = int(np.log2(ch)) - 1
    for i in range(levels):
        pb = [_bf(x) for x in pk]
        if i < levels - 1:
            both = [_dot(x, jnp.concatenate([x, _bf(t)], axis=1)) for x, t in zip(pb, tk)]
            pk = [x[:, :n2] for x in both]
            tk = [t + x[:, n2:] for t, x in zip(tk, both)]
        else:
            tk = [t + _dot(x, _bf(t)) for t, x in zip(tk, pb)]
    s = [s_ref[i, p] for i, p in chains]
    part_s = [_dot_nt(x, _bf(y)) for x, y in zip(ar, s)]
    u = [_bf(_dot(_bf(t), _bf(ps[:n2] + pv[:n2]))) for t, ps, pv in zip(tk, part_s, part_v)]
    y2 = [ps[n2:] + pv[n2:] + _dot(m, uu) for ps, pv, m, uu in zip(part_s, part_v, m_rb, u)]
    s_new = [(ss + _dot_tn(jnp.concatenate([uu, vv], axis=0), y)) * ei[ch - 1:ch, :]
             for ss, uu, vv, y, ei in zip(s, u, v2, bk, e_in)]
    y = [x[:ch] + x[ch:] for x in y2]
    inv = 1.0 / HEAD_DIM
    mu = [_dot(_bf(x), bd) * inv for x in y]
    dy = [x - m for x, m in zip(y, mu)]
    var = [_dot(_bf(x * x), bd) * inv for x in dy]
    bonus = [_dot(_bf(rr * kk * rk_ref[:, sl]), bd) * vv for rr, kk, vv, sl in zip(r, k, v, sls)]
    outs = [(x * lax.rsqrt(vr + LNX_EPS) * lng_ref[:, sl] + lnb_ref[:, sl] + bo) * gg
            for x, vr, bo, sl, gg in zip(dy, var, bonus, sls, g)]
    for (i, p), x in zip(chains, s_new):
        s_ref[i, p] = x
    for i in range(nb):
        y_ref[i] = jnp.concatenate(outs[i * n_pairs:(i + 1) * n_pairs], axis=1)

    @pl.when(ci == pl.num_programs(1) - 1)
    def _():
        so_ref[...] = s_ref[...]


def _rwkv(r, lw, k, v, a, b, g, s0, r_k, lnx_g, lnx_b, tril3, bd, *, t_valid, nb):
    bg, tp, wc = r.shape
    ch = RWKV_CHUNK
    n_pairs = wc // LANES
    grid = (bg // nb, tp // ch)
    blk = lambda b_, c: (b_, c, 0)
    sblk = lambda b_, c: (b_, 0, 0, 0)
    kern = functools.partial(_rwkv_kernel, t_valid=t_valid, ch=ch, n_pairs=n_pairs, nb=nb)
    act = pl.BlockSpec((nb, ch, wc), blk)
    st = pl.BlockSpec((nb, n_pairs, LANES, LANES), sblk)
    return pl.pallas_call(
        kern,
        grid=grid,
        in_specs=[act] * 7 + [st, _full((1, wc)), _full((1, wc)), _full((1, wc)), _full(tril3.shape),
                              _full(bd.shape)],
        out_specs=[act, st],
        out_shape=[jax.ShapeDtypeStruct((bg, tp, wc), f32), jax.ShapeDtypeStruct(s0.shape, f32)],
        scratch_shapes=[pltpu.VMEM((nb, n_pairs, LANES, LANES), f32)],
        compiler_params=_params(("arbitrary", "arbitrary")),
        name="rwkv",
    )(r, lw, k, v, a, b, g, s0, r_k, lnx_g, lnx_b, tril3, bd)


def _state_to_pairs(s):
    b, h, n, _ = s.shape
    s = s.reshape(b, h // 2, 2, n, n)
    eye2 = jnp.eye(2, dtype=s.dtype)
    bd = s[:, :, :, :, None, :] * eye2[None, None, :, None, :, None]
    return bd.reshape(b, h // 2, 2 * n, 2 * n)


def _pairs_to_state(sp, n):
    b, p = sp.shape[:2]
    s = sp.reshape(b, p, 2, n, 2, n)
    s = jnp.stack([s[:, :, 0, :, 0, :], s[:, :, 1, :, 1, :]], axis=2)
    return s.reshape(b, 2 * p, n, n)


def _merge_kernel(x_ref, ya_ref, o_ref, yc_ref, g_ref, wgb_ref, wgc_ref, wbb_ref, wbc_ref, wo_ref, g2_ref, wup_ref,
                  wdn_ref, out_ref):
    x = x_ref[...]
    xn = _bf(_rms(x, g_ref[...]))
    yb = _dot(_bf(o_ref[...]), wbb_ref[...])
    yc = _dot(_bf(yc_ref[...]), wbc_ref[...])
    mixed = ya_ref[...] + jax.nn.sigmoid(_dot(xn, wgb_ref[...])) * yb + jax.nn.sigmoid(_dot(xn, wgc_ref[...])) * yc
    x1 = x + _dot(_bf(mixed), wo_ref[...])
    xn2 = _bf(_rms(x1, g2_ref[...]))
    hid = jnp.square(jnp.maximum(_dot(xn2, wup_ref[...]), 0.0))
    out_ref[...] = x1 + _dot(_bf(hid), wdn_ref[...])


def _merge(x2, ya2, o2, yc2, g, w_gb, w_gc, w_bb, w_bc, w_o, g2, w_up, w_dn, *, tm):
    n, d = x2.shape
    wb = o2.shape[1]
    row = lambda i: (i, 0)
    ws = [w_gb, w_gc, w_bb, w_bc, w_o]
    return pl.pallas_call(
        _merge_kernel,
        grid=(n // tm,),
        in_specs=[pl.BlockSpec((tm, d), row), pl.BlockSpec((tm, d), row), pl.BlockSpec((tm, wb), row),
                  pl.BlockSpec((tm, wb), row), _full((1, d))] + [_full(w.shape) for w in ws]
                 + [_full((1, d)), _full(w_up.shape), _full(w_dn.shape)],
        out_specs=pl.BlockSpec((tm, d), row),
        out_shape=jax.ShapeDtypeStruct((n, d), f32),
        compiler_params=_params(("arbitrary",)),
        name="merge_mlp",
    )(x2, ya2, o2, yc2, g, *ws, g2, w_up, w_dn)


def _layer_weights(l, d, wa, wb, wc, cc, norm_mix_g, w_in, conv_w, q_norm_g, k_norm_g, mu_c, w0, w_w2, a0, w_a2, w_g2,
                   k_k, k_a, r_k, lnx_g, lnx_b, w_br_a, w_br_b, w_br_c, w_o, norm_mlp_g, w_up, w_down):
    s1 = 3 * wa
    s2 = s1 + 3 * wb
    s3 = s2 + cc
    win = w_in[l]
    n_h = wb // HEAD_DIM
    tail = cc - 3 * wc
    zpad = lambda w, before: jnp.pad(w, ((before, tail - before - w.shape[0]), (0, 0)))
    return dict(
        g_mix=norm_mix_g[l].reshape(1, d),
        w_a=_bf(win[:, :s1]), w_b=_bf(win[:, s1:s2]), w_c=_bf(win[:, s2:s3]),
        w_ga=_bf(win[:, s3:s3 + d]), w_gb=_bf(win[:, s3 + d:s3 + 2 * d]), w_gc=_bf(win[:, s3 + 2 * d:]),
        conv_w=conv_w[l],
        qg=jnp.tile(q_norm_g[l], n_h).reshape(1, wb), kg=jnp.tile(k_norm_g[l], n_h).reshape(1, wb),
        mu=mu_c[l].reshape(1, cc), w0=w0[l].reshape(1, wc), a0=a0[l].reshape(1, wc),
        k_k=k_k[l].reshape(1, wc), k_a=k_a[l].reshape(1, wc),
        ww2=_bf(zpad(w_w2[l], 0)), wa2=_bf(zpad(w_a2[l], LORA_W)), wg2=_bf(zpad(w_g2[l], LORA_W + LORA_A)),
        r_k=r_k[l].reshape(1, wc), lnx_g=lnx_g[l].reshape(1, wc), lnx_b=lnx_b[l].reshape(1, wc),
        w_br_a=_bf(w_br_a[l]), w_br_b=_bf(w_br_b[l]), w_br_c=_bf(w_br_c[l]), w_o=_bf(w_o[l]),
        g_mlp=norm_mlp_g[l].reshape(1, d), w_up=_bf(w_up[l]), w_dn=_bf(w_down[l]),
    )


def _pad_time(x, tp):
    return jnp.pad(x, ((0, 0), (0, tp - x.shape[1]), (0, 0)))


def _group_layer(x, w, consts, conv_state, shift_state, wkv_pairs, attn_fn, *, t_valid, bb, tt, tm):
    bg, tp, d = x.shape
    n = bg * tp
    wb = w["w_br_b"].shape[0]
    x2 = x.reshape(n, d)
    ya, conv_new = _mixer_a(x, w["g_mix"], w["w_a"], w["w_ga"], w["conv_w"], w["w_br_a"], conv_state,
                            t_valid=t_valid, bb=bb, tt=tt)
    if bb == 1:
        q16, k16, v16, k32, v32 = _proj_b(x, w["g_mix"], w["w_b"], w["qg"], w["kg"], consts["bd256"], tt=tt,
                                          t_out=t_valid)
    else:
        flat = _proj_b(x.reshape(1, n, d), w["g_mix"], w["w_b"], w["qg"], w["kg"], consts["bd256"], tt=tm, t_out=n)
        q16, k16, v16, k32, v32 = (t.reshape(bg, tp, wb) for t in flat)
        k32, v32 = k32[:, :t_valid], v32[:, :t_valid]
    o = attn_fn(q16, k16, v16)
    r, lw, k, v, a, b, g, shift_new = _proj_c(x, w["g_mix"], w["w_c"], w["mu"], w["w0"], w["a0"], w["k_k"], w["k_a"],
                                              w["ww2"], w["wa2"], w["wg2"], consts["bd256"], shift_state,
                                              t_valid=t_valid, bb=bb, tt=tt)
    ch = RWKV_CHUNK
    tpr = -(-tp // ch) * ch
    ops = [r, lw, k, v, a, b, g]
    if tpr != tp:
        ops = [_pad_time(t, tpr) for t in ops]
    yc, wkv_new = _rwkv(*ops, wkv_pairs, w["r_k"], w["lnx_g"], w["lnx_b"], consts["tril3"], consts["bd128"],
                        t_valid=t_valid, nb=_pick_tile(bg, RWKV_SEQS, 1))
    yc = yc[:, :tp]
    x_new = _merge(x2, ya.reshape(n, d), o.reshape(n, wb), yc.reshape(n, wb), w["g_mix"], w["w_gb"], w["w_gc"],
                   w["w_br_b"], w["w_br_c"], w["w_o"], w["g_mlp"], w["w_up"], w["w_dn"], tm=tm)
    return x_new.reshape(bg, tp, d), conv_new, shift_new, wkv_new, k32, v32


def kernel(x_prompt, x_sample, cache_k, cache_v, state_conv, state_shift, state_wkv, page_table, meta_tokens, norm_mix_g, w_in, conv_w, q_norm_g, k_norm_g, sb_bias, mu_c, w0, w_w2, a0, w_a2, w_g2, k_k, k_a, r_k, lnx_g, lnx_b, w_br_a, w_br_b, w_br_c, w_o, norm_mlp_g, w_up, w_down):
    bp, seq, d = x_prompt.shape
    db, dseq, _ = x_sample.shape
    depth = w_in.shape[0]
    wa = w_br_a.shape[1]
    wb = w_br_b.shape[1]
    wc = w_br_c.shape[1]
    cc = mu_c.shape[1]
    n_h = wb // HEAD_DIM
    n_hc = wc // HEAD_DIM
    n_pool, page = cache_k.shape[1], cache_k.shape[2]
    assert page == PAGE and cache_k.shape[3] == n_h and cache_k.shape[4] == HEAD_DIM
    assert n_h & (n_h - 1) == 0 and dseq * n_h <= LANES
    n_pages = page_table.shape[1]

    tp_valid = seq + N_META
    tp_p = -(-tp_valid // LANES) * LANES
    tp_s = -(-dseq // 8) * 8
    meta = jnp.broadcast_to(meta_tokens.astype(x_prompt.dtype)[None], (bp, N_META, d))
    xp = jnp.concatenate([meta, x_prompt, jnp.zeros((bp, tp_p - tp_valid, d), x_prompt.dtype)], axis=1)
    xs = _pad_time(x_sample, tp_s)

    ks = ATTN_KEYS
    tq = _pick_tile(tp_p, ATTN_QUERIES, ks)
    tri = np.tril(np.ones((ks, ks)))
    tril = np.tril(np.ones((RWKV_CHUNK, RWKV_CHUNK)))
    consts = dict(
        bd256=_block_diag_ones(2 * LANES, HEAD_DIM),
        bd128=_block_diag_ones(LANES, HEAD_DIM),
        tril3=jnp.asarray(np.concatenate([tril] * 3, axis=1), dtype=bf16),
        tri2=jnp.asarray(np.kron(np.eye(2), tri), dtype=bf16),
    )
    ck = jnp.transpose(cache_k, (0, 1, 3, 4, 2))
    cv = jnp.transpose(cache_v, (0, 1, 3, 4, 2))
    rows = dseq * n_h
    head_mask = jnp.asarray((np.arange(rows)[:, None] % n_h) == (np.arange(wb)[None, :] // HEAD_DIM), dtype=f32)
    n_group = _pick_tile(n_pages, 16, 1)

    tt_p = _pick_tile(tp_p, 528, 16)
    tm_p = _pick_tile(bp * tp_p, 512, 16)
    tm_s = _pick_tile(db * tp_s, 512, 8)

    zeros_conv = jnp.zeros((bp, CONV_W - 1, wa), f32)
    zeros_shift = jnp.zeros((bp, 1, cc), f32)
    zeros_wkv = jnp.zeros((bp, n_hc // 2, LANES, LANES), f32)

    outs = [[] for _ in range(10)]
    for l in range(depth):
        w = _layer_weights(l, d, wa, wb, wc, cc, norm_mix_g, w_in, conv_w, q_norm_g, k_norm_g, mu_c, w0, w_w2, a0,
                           w_a2, w_g2, k_k, k_a, r_k, lnx_g, lnx_b, w_br_a, w_br_b, w_br_c, w_o, norm_mlp_g, w_up,
                           w_down)
        bias2 = sb_bias[l].astype(f32) * LOG2E
        bias_pairs = jnp.repeat(bias2.reshape(n_h // 2, 1, 2), ks, axis=2)

        attn_p = lambda q, k, v: _attn_prompt(q, k, v, bias_pairs, consts["tri2"], tq=tq)
        xp, cp, sp, wp, kp, vp = _group_layer(xp, w, consts, zeros_conv, zeros_shift, zeros_wkv, attn_p,
                                              t_valid=tp_valid, bb=1, tt=tt_p, tm=tm_p)

        def attn_s(q, k, v, l=l):
            q4 = q[:, :dseq].reshape(db, dseq, 1, n_h, HEAD_DIM)
            qr = (q4 * jnp.eye(n_h, dtype=q.dtype)[None, None, :, :, None]).reshape(db, rows, wb)
            as_page = lambda t: jnp.pad(jnp.swapaxes(t[:, :dseq], 1, 2), ((0, 0), (0, 0), (0, LANES - dseq)))
            bias_rows = jnp.broadcast_to(jnp.tile(bias2, dseq)[:, None], (rows, LANES))
            o = _attn_sample(qr, as_page(k), as_page(v), bias_rows, head_mask, ck, cv, page_table, l,
                             n_group=n_group, n_heads=n_h)
            return _pad_time(o, tp_s)

        xs, cs, ss, ws, ks_new, vs_new = _group_layer(xs, w, consts, state_conv[l], state_shift[l][:, None, :],
                                                      _state_to_pairs(state_wkv[l]), attn_s,
                                                      t_valid=dseq, bb=db, tt=tp_s, tm=tm_s)
        heads = lambda t: t.reshape(t.shape[0], t.shape[1], n_h, HEAD_DIM)
        vals = (heads(kp), heads(vp), heads(ks_new), heads(vs_new),
                cp, cs, sp[:, 0], ss[:, 0], _pairs_to_state(wp, HEAD_DIM), _pairs_to_state(ws, HEAD_DIM))
        for lst, val in zip(outs, vals):
            lst.append(val)
    return (xp[:, N_META:tp_valid], xs[:, :dseq]) + tuple(jnp.stack(lst) for lst in outs)
```

```python
import functools

import numpy as np
import jax
import jax.numpy as jnp
from jax import lax
from jax.experimental import pallas as pl
from jax.experimental.pallas import tpu as pltpu

HEAD_DIM = 64
N_META = 16
CONV_W = 3
LORA_W = 64
LORA_A = 64
LORA_G = 128
NORM_EPS = 1e-6
LNX_EPS = 64e-5
LOG2E = 1.4426950408889634
LANES = 128
RWKV_CHUNK = 64
RWKV_SEQS = 2
ATTN_KEYS = 128
ATTN_QUERIES = 384
KEY_BLOCKS = 2
PAGE = 128
VMEM_LIMIT = 56 * 1024 * 1024

f32 = jnp.float32
bf16 = jnp.bfloat16


def _bf(x):
    return x.astype(bf16)


def _dot(a, b):
    return jnp.dot(a, b, preferred_element_type=f32)


def _dot_nt(a, b):
    return lax.dot_general(a, b, (((1,), (1,)), ((), ())), preferred_element_type=f32)


def _dot_tn(a, b):
    return lax.dot_general(a, b, (((0,), (0,)), ((), ())), preferred_element_type=f32)


def _split2(x):
    hi = x.astype(bf16)
    lo = (x - hi.astype(f32)).astype(bf16)
    return hi, lo


def _split3(x):
    hi = x.astype(bf16)
    r1 = x - hi.astype(f32)
    mid = r1.astype(bf16)
    lo = (r1 - mid.astype(f32)).astype(bf16)
    return hi, mid, lo


def _dot_sel(x, sel):
    hi, lo = _split2(x)
    return _dot(hi, sel) + _dot(lo, sel)


def _rms(x, g):
    ms = jnp.mean(x * x, axis=-1, keepdims=True)
    return x * lax.rsqrt(ms + NORM_EPS) * g


def _segsum(x, bd):
    w = bd.shape[0]
    parts = [_dot_sel(x[:, s:s + w], bd) for s in range(0, x.shape[1], w)]
    return parts[0] if len(parts) == 1 else jnp.concatenate(parts, axis=1)


def _neg_abs(z):
    bits = lax.bitcast_convert_type(z, jnp.int32) | jnp.int32(-2147483648)
    return lax.bitcast_convert_type(bits, f32)


def _softplus2(z2):
    return jnp.maximum(z2, 0.0) + jnp.log(1.0 + jnp.exp2(_neg_abs(z2))) * LOG2E


def _pick_tile(n, target, mult):
    best = None
    for t in range(mult, min(n, target) + 1, mult):
        if n % t == 0:
            best = t
    if best is None:
        assert n <= target, (n, target, mult)
        best = n
    return best


def _block_diag_ones(n, blk):
    i = np.arange(n)
    return jnp.asarray((i[:, None] // blk) == (i[None, :] // blk), dtype=bf16)


def _full(shape):
    nd = len(shape)
    return pl.BlockSpec(shape, lambda *_: (0,) * nd)


def _params(sem):
    return pltpu.CompilerParams(dimension_semantics=sem, vmem_limit_bytes=VMEM_LIMIT)


def _shift_down(v, k, prev, tpos):
    out = pltpu.roll(v, k, 0)
    for j in range(k):
        out = jnp.where(tpos == j, prev[j], out)
    return out


def _mixer_a_kernel(x_ref, g_ref, wa_ref, wg_ref, cw_ref, wbr_ref, st_ref, y_ref, ns_ref, carry_ref,
                    *, bb, tt, wa, t_valid):
    ti = pl.program_id(1)

    @pl.when(ti == 0)
    def _():
        carry_ref[...] = st_ref[...]

    d = x_ref.shape[-1]
    x = x_ref[...].reshape(bb * tt, d)
    xn = _bf(_rms(x, g_ref[...]))
    pa = _dot(xn, wa_ref[...])
    gb, gc, h = pa[:, :wa], pa[:, wa:2 * wa], pa[:, 2 * wa:]
    u = gc * h
    tpos = lax.broadcasted_iota(jnp.int32, (bb, tt, wa), 1).reshape(bb * tt, wa)
    carry = carry_ref[...]
    c0 = jnp.broadcast_to(carry[:, 0:1, :], (bb, tt, wa)).reshape(bb * tt, wa)
    c1 = jnp.broadcast_to(carry[:, 1:2, :], (bb, tt, wa)).reshape(bb * tt, wa)
    u1 = _shift_down(u, 1, (c1,), tpos)
    u2 = _shift_down(u, 2, (c0, c1), tpos)
    cw = cw_ref[...]
    conv = cw[0:1] * u2 + cw[1:2] * u1 + cw[2:3] * u
    ya = _dot(_bf(gb * conv), wbr_ref[...])
    gate = jax.nn.sigmoid(_dot(xn, wg_ref[...]))
    y_ref[...] = (gate * ya).reshape(bb, tt, d)
    u3 = u.reshape(bb, tt, wa)
    carry_ref[...] = u3[:, tt - 2:tt, :]

    lo = t_valid - 2
    @pl.when(ti == lo // tt)
    def _():
        ns_ref[...] = u3[:, lo % tt:lo % tt + 2, :]


def _mixer_a(x, g, w_a, w_ga, conv_w, w_br, state, *, t_valid, bb, tt):
    bg, tp, d = x.shape
    wa = w_br.shape[0]
    assert (t_valid - 2) // tt == (t_valid - 1) // tt
    grid = (bg // bb, tp // tt)
    kern = functools.partial(_mixer_a_kernel, bb=bb, tt=tt, wa=wa, t_valid=t_valid)
    return pl.pallas_call(
        kern,
        grid=grid,
        in_specs=[
            pl.BlockSpec((bb, tt, d), lambda b, t: (b, t, 0)),
            _full((1, d)), _full(w_a.shape), _full(w_ga.shape), _full(conv_w.shape), _full(w_br.shape),
            pl.BlockSpec((bb, 2, wa), lambda b, t: (b, 0, 0)),
        ],
        out_specs=[
            pl.BlockSpec((bb, tt, d), lambda b, t: (b, t, 0)),
            pl.BlockSpec((bb, 2, wa), lambda b, t: (b, 0, 0)),
        ],
        out_shape=[jax.ShapeDtypeStruct((bg, tp, d), f32), jax.ShapeDtypeStruct((bg, 2, wa), f32)],
        scratch_shapes=[pltpu.VMEM((bb, 2, wa), f32)],
        compiler_params=_params(("arbitrary", "arbitrary")),
        name="mixer_a",
    )(x, g, w_a, w_ga, conv_w, w_br, state)


def _proj_b_kernel(x_ref, g_ref, wb_ref, qg_ref, kg_ref, bd_ref, q16_ref, k16_ref, v16_ref, k_ref, v_ref, *, wb):
    xn = _bf(_rms(x_ref[0], g_ref[...]))
    pb = _dot(xn, wb_ref[...])
    q, k, v = pb[:, :wb], pb[:, wb:2 * wb], pb[:, 2 * wb:]
    bd = bd_ref[...]
    inv = 1.0 / HEAD_DIM
    qn = q * lax.rsqrt(_segsum(q * q, bd) * inv + NORM_EPS) * qg_ref[...]
    kn = k * lax.rsqrt(_segsum(k * k, bd) * inv + NORM_EPS) * kg_ref[...]
    q16_ref[0] = _bf(qn * (LOG2E * HEAD_DIM ** -0.5))
    k16_ref[0] = _bf(kn)
    v16_ref[0] = _bf(v)
    k_ref[0] = kn
    v_ref[0] = v


def _proj_b(x, g, w_b, qg, kg, bd, *, tt, t_out):
    bg, tp, d = x.shape
    wb = w_b.shape[1] // 3
    blk = lambda b, t: (b, t, 0)
    kern = functools.partial(_proj_b_kernel, wb=wb)
    return pl.pallas_call(
        kern,
        grid=(bg, tp // tt),
        in_specs=[pl.BlockSpec((1, tt, d), blk), _full((1, d)), _full(w_b.shape), _full((1, wb)), _full((1, wb)),
                  _full(bd.shape)],
        out_specs=[pl.BlockSpec((1, tt, wb), blk)] * 5,
        out_shape=[jax.ShapeDtypeStruct((bg, tp, wb), bf16)] * 3 + [jax.ShapeDtypeStruct((bg, t_out, wb), f32)] * 2,
        compiler_params=_params(("arbitrary", "arbitrary")),
        name="proj_b",
    )(x, g, w_b, qg, kg, bd)


def _attn_prompt_kernel(bias_ref, q_ref, k_ref, v_ref, tri_ref, o_ref, *, tq, ks):
    qi = pl.program_id(2)
    q2 = q_ref[0]
    bias = bias_ref[0]
    tri2 = tri_ref[...]
    n_sub = tq // ks
    first = lax.broadcasted_iota(jnp.int32, (ks, LANES), 1) < HEAD_DIM
    row = lax.broadcasted_iota(jnp.int32, (tq, 2 * ks), 0)
    key = lax.broadcasted_iota(jnp.int32, (tq, 2 * ks), 1) & (ks - 1)

    def stack_heads(x):
        zero = jnp.zeros_like(x)
        return jnp.concatenate([jnp.where(first, x, zero), jnp.where(first, zero, x)], axis=0)

    def sub_block(start, c, acc, mask=None, lo=0):
        n = tq - lo
        k_st = stack_heads(k_ref[0, pl.ds(start, ks), :])
        v_st = stack_heads(v_ref[0, pl.ds(start, ks), :])
        z = _dot_nt(q2[lo:], k_st) + bias
        nl = _softplus2(z)
        if mask is not None:
            nl = jnp.where(mask[lo:], nl, 0.0)
        incl = _dot(_bf(nl), tri2)
        p = jnp.exp2(z - incl - c[lo:])
        if mask is not None:
            p = jnp.where(mask[lo:], p, 0.0)
        acc_lo = acc[lo:] + _dot(_bf(p), v_st)
        c_lo = c[lo:] + jnp.concatenate([jnp.broadcast_to(incl[:, 0:1], (n, ks)),
                                         jnp.broadcast_to(incl[:, ks:ks + 1], (n, ks))], axis=1)
        if lo:
            c_lo = jnp.concatenate([c[:lo], c_lo], axis=0)
            acc_lo = jnp.concatenate([acc[:lo], acc_lo], axis=0)
        return c_lo, acc_lo

    def key_blocks(start, n, c, acc):
        for j in reversed(range(n)):
            c, acc = sub_block(pl.multiple_of(start + j * ks, ks), c, acc)
        return c, acc

    left = qi % KEY_BLOCKS

    def head(extra):
        c = jnp.zeros((tq, 2 * ks), f32)
        acc = jnp.zeros((tq, LANES), f32)
        for j in reversed(range(n_sub)):
            c, acc = sub_block(pl.multiple_of(qi * tq + j * ks, ks), c, acc, (key + j * ks) < row, j * ks)
        return key_blocks((qi - extra) * tq, extra * n_sub, c, acc)

    assert KEY_BLOCKS in (1, 2)
    c, acc = head(0) if KEY_BLOCKS == 1 else lax.cond(left == 1, lambda: head(1), lambda: head(0))

    def body(n, carry):
        return key_blocks((qi - left - KEY_BLOCKS * (n + 1)) * tq, KEY_BLOCKS * n_sub, carry[0], carry[1])

    c, acc = lax.fori_loop(0, qi // KEY_BLOCKS, body, (c, acc))
    o_ref[0] = acc


def _attn_prompt(q16, k16, v16, bias_pairs, tri2, *, tq):
    bg, tp, wb = q16.shape
    ks = tri2.shape[0] // 2
    grid = (bg, wb // LANES, tp // tq)
    kern = functools.partial(_attn_prompt_kernel, tq=tq, ks=ks)
    return pl.pallas_call(
        kern,
        grid=grid,
        in_specs=[
            pl.BlockSpec((1, 1, 2 * ks), lambda b, h, i: (h, 0, 0)),
            pl.BlockSpec((1, tq, LANES), lambda b, h, i: (b, i, h)),
            pl.BlockSpec((1, tp, LANES), lambda b, h, i: (b, 0, h)),
            pl.BlockSpec((1, tp, LANES), lambda b, h, i: (b, 0, h)),
            _full(tri2.shape),
        ],
        out_specs=pl.BlockSpec((1, tq, LANES), lambda b, h, i: (b, i, h)),
        out_shape=jax.ShapeDtypeStruct((bg, tp, wb), f32),
        compiler_params=_params(("arbitrary", "arbitrary", "arbitrary")),
        name="attn_prompt",
    )(bias_pairs, q16, k16, v16, tri2)


def _scan_lanes(x):
    n = x.shape[1]
    lane = lax.broadcasted_iota(jnp.int32, x.shape, 1)
    d = 1
    while d < n:
        x = x + jnp.where(lane < n - d, pltpu.roll(x, n - d, 1), 0.0)
        d *= 2
    return x


def _attn_sample_kernel(pt_ref, q_ref, kn_ref, vn_ref, bias_ref, hm_ref, *rest, n_group, n_heads):
    kp_refs = rest[:n_group]
    vp_refs = rest[n_group:2 * n_group]
    o_ref, acc_ref, c_ref = rest[2 * n_group:]
    si = pl.program_id(1)
    q = q_ref[0]
    rows, w = q.shape
    bias = bias_ref[...]

    def block(kt, vt, c, acc, mask):
        reps = kt.shape[1] // LANES
        z = _dot(q, kt) + jnp.concatenate([bias] * reps, axis=1)
        nl = _softplus2(z)
        if mask is not None:
            nl = jnp.where(mask, nl, 0.0)
        incl = _scan_lanes(nl)
        p = jnp.exp2(z - incl - jnp.concatenate([c] * reps, axis=1))
        if mask is not None:
            p = jnp.where(mask, p, 0.0)
        return c + incl[:, 0:1], acc + _dot_nt(_bf(p), vt)

    @pl.when(si == 0)
    def _():
        trow = lax.broadcasted_iota(jnp.int32, (rows, LANES), 0) // n_heads
        pos = lax.broadcasted_iota(jnp.int32, (rows, LANES), 1)
        c, acc = block(kn_ref[0], vn_ref[0], jnp.zeros(c_ref.shape, f32), jnp.zeros(acc_ref.shape, f32), pos < trow)
        c_ref[...] = c
        acc_ref[...] = acc

    def pages(refs):
        return jnp.concatenate([_bf(r[0, 0].reshape(w, PAGE)) for r in reversed(refs)], axis=1)

    c, acc = block(pages(kp_refs), pages(vp_refs), c_ref[...], acc_ref[...], None)
    c_ref[...] = c
    acc_ref[...] = acc

    @pl.when(si == pl.num_programs(1) - 1)
    def _():
        own = jnp.where(hm_ref[...] > 0, acc, 0.0).reshape(rows // n_heads, n_heads, w)
        o_ref[0] = jnp.sum(own, axis=1)


def _attn_sample(q_rows, k_new, v_new, bias_rows, head_mask, cache_k, cache_v, page_table, layer, *, n_group,
                 n_heads):
    db, rows, wb = q_rows.shape
    n_pages = page_table.shape[1]
    n_steps = n_pages // n_group
    page_blk = (1, 1) + cache_k.shape[2:]

    def page_map(g):
        def f(b, s, pt):
            return (layer, pt[b, n_pages - 1 - (s * n_group + g)], 0, 0, 0)
        return f

    page_specs = [pl.BlockSpec(page_blk, page_map(g)) for g in range(n_group)]
    seq = lambda b, s, pt: (b, 0, 0)
    const2 = lambda b, s, pt: (0, 0)
    t_rows = rows // n_heads
    kern = functools.partial(_attn_sample_kernel, n_group=n_group, n_heads=n_heads)
    gs = pltpu.PrefetchScalarGridSpec(
        num_scalar_prefetch=1,
        grid=(db, n_steps),
        in_specs=[
            pl.BlockSpec((1, rows, wb), seq),
            pl.BlockSpec((1,) + k_new.shape[1:], seq),
            pl.BlockSpec((1,) + v_new.shape[1:], seq),
            pl.BlockSpec(bias_rows.shape, const2),
            pl.BlockSpec(head_mask.shape, const2),
        ] + page_specs + page_specs,
        out_specs=pl.BlockSpec((1, t_rows, wb), seq),
        scratch_shapes=[pltpu.VMEM((rows, wb), f32), pltpu.VMEM((rows, LANES), f32)],
    )
    return pl.pallas_call(
        kern,
        grid_spec=gs,
        out_shape=jax.ShapeDtypeStruct((db, t_rows, wb), f32),
        compiler_params=_params(("arbitrary", "arbitrary")),
        name="attn_sample",
    )(page_table, q_rows, k_new, v_new, bias_rows, head_mask, *([cache_k] * n_group), *([cache_v] * n_group))


def _proj_c_kernel(x_ref, g_ref, wc_ref, mu_ref, w0_ref, a0_ref, kk_ref, ka_ref, ww2_ref, wa2_ref, wg2_ref, bd_ref,
                   st_ref, r_ref, lw_ref, k_ref, v_ref, a_ref, b_ref, gg_ref, ns_ref, carry_ref,
                   *, bb, tt, wc, t_valid):
    ti = pl.program_id(1)

    @pl.when(ti == 0)
    def _():
        carry_ref[...] = st_ref[...]

    d = x_ref.shape[-1]
    cc = wc_ref.shape[1]
    x = x_ref[...].reshape(bb * tt, d)
    xn = _bf(_rms(x, g_ref[...]))
    pc = _dot(xn, wc_ref[...])
    tpos = lax.broadcasted_iota(jnp.int32, (bb, tt, cc), 1).reshape(bb * tt, cc)
    cprev = jnp.broadcast_to(carry_ref[...], (bb, tt, cc)).reshape(bb * tt, cc)
    prev = _shift_down(pc, 1, (cprev,), tpos)
    pcs = pc + (prev - pc) * mu_ref[...]
    r, kc, vc, lo = pcs[:, :wc], pcs[:, wc:2 * wc], pcs[:, 2 * wc:3 * wc], pcs[:, 3 * wc:]
    w_log = -jax.nn.softplus(-(w0_ref[...] + _dot(_bf(jnp.tanh(lo)), ww2_ref[...]))) - 0.5
    a = jax.nn.sigmoid(a0_ref[...] + _dot(_bf(lo), wa2_ref[...]))
    g = _dot(_bf(jax.nn.sigmoid(lo)), wg2_ref[...])
    kk = kc * kk_ref[...]
    kk = kk / jnp.maximum(jnp.sqrt(_segsum(kk * kk, bd_ref[...])), 1e-12)
    kmod = kc * (1.0 + (a - 1.0) * ka_ref[...])
    shp = (bb, tt, wc)
    lw_ref[...] = (-jnp.exp(w_log)).reshape(shp)
    for ref, val in ((r_ref, r), (k_ref, kmod), (v_ref, vc), (a_ref, -kk), (b_ref, kk * a), (gg_ref, g)):
        ref[...] = val.astype(ref.dtype).reshape(shp)
    pc3 = pc.reshape(bb, tt, cc)
    carry_ref[...] = pc3[:, tt - 1:tt, :]

    last = t_valid - 1
    @pl.when(ti == last // tt)
    def _():
        ns_ref[...] = pc3[:, last % tt:last % tt + 1, :]


def _proj_c(x, g, w_c, mu, w0, a0, k_k, k_a, ww2, wa2, wg2, bd, state, *, t_valid, bb, tt):
    bg, tp, d = x.shape
    cc = w_c.shape[1]
    wc = w0.shape[1]
    grid = (bg // bb, tp // tt)
    blk = lambda b, t: (b, t, 0)
    kern = functools.partial(_proj_c_kernel, bb=bb, tt=tt, wc=wc, t_valid=t_valid)
    act = jax.ShapeDtypeStruct((bg, tp, wc), bf16 if tt % 16 == 0 else f32)
    lw_act = jax.ShapeDtypeStruct((bg, tp, wc), f32)
    return pl.pallas_call(
        kern,
        grid=grid,
        in_specs=[
            pl.BlockSpec((bb, tt, d), blk),
            _full((1, d)), _full(w_c.shape), _full((1, cc)), _full((1, wc)), _full((1, wc)), _full((1, wc)),
            _full((1, wc)), _full(ww2.shape), _full(wa2.shape), _full(wg2.shape), _full(bd.shape),
            pl.BlockSpec((bb, 1, cc), lambda b, t: (b, 0, 0)),
        ],
        out_specs=[pl.BlockSpec((bb, tt, wc), blk)] * 7 + [pl.BlockSpec((bb, 1, cc), lambda b, t: (b, 0, 0))],
        out_shape=[act, lw_act] + [act] * 5 + [jax.ShapeDtypeStruct((bg, 1, cc), f32)],
        scratch_shapes=[pltpu.VMEM((bb, 1, cc), f32)],
        compiler_params=_params(("arbitrary", "arbitrary")),
        name="proj_c",
    )(x, g, w_c, mu, w0, a0, k_k, k_a, ww2, wa2, wg2, bd, state)


def _rwkv_kernel(r_ref, lw_ref, k_ref, v_ref, a_ref, b_ref, g_ref, s0_ref, rk_ref, lng_ref, lnb_ref, tril3_ref,
                 bd_ref, y_ref, so_ref, s_ref, *, t_valid, ch, n_pairs, nb):
    ci = pl.program_id(1)

    @pl.when(ci == 0)
    def _():
        s_ref[...] = s0_ref[...]

    n2 = 2 * ch
    tpos = lax.broadcasted_iota(jnp.int32, (ch, LANES), 0) + ci * ch
    valid = tpos < t_valid
    first = lax.broadcasted_iota(jnp.int32, (ch, LANES), 1) < HEAD_DIM
    row = lax.broadcasted_iota(jnp.int32, (n2, n2), 0)
    col = lax.broadcasted_iota(jnp.int32, (n2, n2), 1)
    strict = row > col
    incl = row >= col
    eye = (row == col).astype(f32)
    tril3 = tril3_ref[...]
    bd = bd_ref[...]
    zb = jnp.zeros((ch, LANES), bf16)

    def stack(x):
        xb = _bf(x)
        return jnp.concatenate([jnp.where(first, xb, zb), jnp.where(first, zb, xb)], axis=0)

    chains = [(i, p) for i in range(nb) for p in range(n_pairs)]
    sls = [slice(p * LANES, (p + 1) * LANES) for _, p in chains]
    ld = lambda ref: [ref[i, :, sl].astype(f32) for (i, _), sl in zip(chains, sls)]
    lw = [jnp.where(valid, x, 0.0) for x in ld(lw_ref)]
    cum = [_dot(tril3, jnp.concatenate(_split3(x), axis=0)) for x in lw]
    e_in = [jnp.exp(c) for c in cum]
    e_neg = [jnp.exp(-c) for c in cum]
    r, k, v, a, b, g = ld(r_ref), ld(k_ref), ld(v_ref), ld(a_ref), ld(b_ref), ld(g_ref)
    v2 = [stack(x) for x in v]
    ar = [jnp.concatenate([stack(aa * jnp.exp(c - x)), stack(rr * ei)], axis=0)
          for aa, c, x, rr, ei in zip(a, cum, lw, r, e_in)]
    bk = [jnp.concatenate([stack(jnp.where(valid, bb, 0.0) * en), stack(jnp.where(valid, kk, 0.0) * en)], axis=0)
          for bb, kk, en in zip(b, k, e_neg)]
    gm = [_dot_nt(x, y) for x, y in zip(ar, bk)]
    l_ab = [jnp.where(strict, g[:n2, :n2], 0.0) for g in gm]
    lm = [_bf(jnp.concatenate([jnp.where(strict, g[:n2, n2:], 0.0), jnp.where(incl, g[n2:, n2:], 0.0)], axis=0))
          for g in gm]
    m_rb = [_bf(jnp.where(incl, g[n2:, :n2], 0.0)) for g in gm]
    part_v = [_dot(x, y) for x, y in zip(lm, v2)]
    tk = [eye + x for x in l_ab]
    pk = [_dot(_bf(x), _bf(x)) for x in l_ab]
    levels = int(np.log2(ch)) - 1
    for i in range(levels):
        pb = [_bf(x) for x in pk]
        if i < levels - 1:
            both = [_dot(x, jnp.concatenate([x, _bf(t)], axis=1)) for x, t in zip(pb, tk)]
            pk = [x[:, :n2] for x in both]
            tk = [t + x[:, n2:] for t, x in zip(tk, both)]
        else:
            tk = [t + _dot(x, _bf(t)) for t, x in zip(tk, pb)]
    s = [s_ref[i, p] for i, p in chains]
    part_s = [_dot_nt(x, _bf(y)) for x, y in zip(ar, s)]
    u = [_bf(_dot(_bf(t), _bf(ps[:n2] + pv[:n2]))) for t, ps, pv in zip(tk, part_s, part_v)]
    y2 = [ps[n2:] + pv[n2:] + _dot(m, uu) for ps, pv, m, uu in zip(part_s, part_v, m_rb, u)]
    s_new = [(ss + _dot_tn(jnp.concatenate([uu, vv], axis=0), y)) * ei[ch - 1:ch, :]
             for ss, uu, vv, y, ei in zip(s, u, v2, bk, e_in)]
    y = [x[:ch] + x[ch:] for x in y2]
    inv = 1.0 / HEAD_DIM
    mu = [_dot(_bf(x), bd) * inv for x in y]
    dy = [x - m for x, m in zip(y, mu)]
    var = [_dot(_bf(x * x), bd) * inv for x in dy]
    bonus = [_dot(_bf(rr * kk * rk_ref[:, sl]), bd) * vv for rr, kk, vv, sl in zip(r, k, v, sls)]
    outs = [(x * lax.rsqrt(vr + LNX_EPS) * lng_ref[:, sl] + lnb_ref[:, sl] + bo) * gg
            for x, vr, bo, sl, gg in zip(dy, var, bonus, sls, g)]
    for (i, p), x in zip(chains, s_new):
        s_ref[i, p] = x
    for i in range(nb):
        y_ref[i] = jnp.concatenate(outs[i * n_pairs:(i + 1) * n_pairs], axis=1)

    @pl.when(ci == pl.num_programs(1) - 1)
    def _():
        so_ref[...] = s_ref[...]


def _rwkv(r, lw, k, v, a, b, g, s0, r_k, lnx_g, lnx_b, tril3, bd, *, t_valid, nb):
    bg, tp, wc = r.shape
    ch = RWKV_CHUNK
    n_pairs = wc // LANES
    grid = (bg // nb, tp // ch)
    blk = lambda b_, c: (b_, c, 0)
    sblk = lambda b_, c: (b_, 0, 0, 0)
    kern = functools.partial(_rwkv_kernel, t_valid=t_valid, ch=ch, n_pairs=n_pairs, nb=nb)
    act = pl.BlockSpec((nb, ch, wc), blk)
    st = pl.BlockSpec((nb, n_pairs, LANES, LANES), sblk)
    return pl.pallas_call(
        kern,
        grid=grid,
        in_specs=[act] * 7 + [st, _full((1, wc)), _full((1, wc)), _full((1, wc)), _full(tril3.shape),
                              _full(bd.shape)],
        out_specs=[act, st],
        out_shape=[jax.ShapeDtypeStruct((bg, tp, wc), f32), jax.ShapeDtypeStruct(s0.shape, f32)],
        scratch_shapes=[pltpu.VMEM((nb, n_pairs, LANES, LANES), f32)],
        compiler_params=_params(("arbitrary", "arbitrary")),
        name="rwkv",
    )(r, lw, k, v, a, b, g, s0, r_k, lnx_g, lnx_b, tril3, bd)


def _state_to_pairs(s):
    b, h, n, _ = s.shape
    s = s.reshape(b, h // 2, 2, n, n)
    eye2 = jnp.eye(2, dtype=s.dtype)
    bd = s[:, :, :, :, None, :] * eye2[None, None, :, None, :, None]
    return bd.reshape(b, h // 2, 2 * n, 2 * n)


def _pairs_to_state(sp, n):
    b, p = sp.shape[:2]
    s = sp.reshape(b, p, 2, n, 2, n)
    s = jnp.stack([s[:, :, 0, :, 0, :], s[:, :, 1, :, 1, :]], axis=2)
    return s.reshape(b, 2 * p, n, n)


def _merge_kernel(x_ref, ya_ref, o_ref, yc_ref, g_ref, wgb_ref, wgc_ref, wbb_ref, wbc_ref, wo_ref, g2_ref, wup_ref,
                  wdn_ref, out_ref):
    x = x_ref[...]
    xn = _bf(_rms(x, g_ref[...]))
    yb = _dot(_bf(o_ref[...]), wbb_ref[...])
    yc = _dot(_bf(yc_ref[...]), wbc_ref[...])
    mixed = ya_ref[...] + jax.nn.sigmoid(_dot(xn, wgb_ref[...])) * yb + jax.nn.sigmoid(_dot(xn, wgc_ref[...])) * yc
    x1 = x + _dot(_bf(mixed), wo_ref[...])
    xn2 = _bf(_rms(x1, g2_ref[...]))
    hid = jnp.square(jnp.maximum(_dot(xn2, wup_ref[...]), 0.0))
    out_ref[...] = x1 + _dot(_bf(hid), wdn_ref[...])


def _merge(x2, ya2, o2, yc2, g, w_gb, w_gc, w_bb, w_bc, w_o, g2, w_up, w_dn, *, tm):
    n, d = x2.shape
    wb = o2.shape[1]
    row = lambda i: (i, 0)
    ws = [w_gb, w_gc, w_bb, w_bc, w_o]
    return pl.pallas_call(
        _merge_kernel,
        grid=(n // tm,),
        in_specs=[pl.BlockSpec((tm, d), row), pl.BlockSpec((tm, d), row), pl.BlockSpec((tm, wb), row),
                  pl.BlockSpec((tm, wb), row), _full((1, d))] + [_full(w.shape) for w in ws]
                 + [_full((1, d)), _full(w_up.shape), _full(w_dn.shape)],
        out_specs=pl.BlockSpec((tm, d), row),
        out_shape=jax.ShapeDtypeStruct((n, d), f32),
        compiler_params=_params(("arbitrary",)),
        name="merge_mlp",
    )(x2, ya2, o2, yc2, g, *ws, g2, w_up, w_dn)


def _layer_weights(l, d, wa, wb, wc, cc, norm_mix_g, w_in, conv_w, q_norm_g, k_norm_g, mu_c, w0, w_w2, a0, w_a2, w_g2,
                   k_k, k_a, r_k, lnx_g, lnx_b, w_br_a, w_br_b, w_br_c, w_o, norm_mlp_g, w_up, w_down):
    s1 = 3 * wa
    s2 = s1 + 3 * wb
    s3 = s2 + cc
    win = w_in[l]
    n_h = wb // HEAD_DIM
    tail = cc - 3 * wc
    zpad = lambda w, before: jnp.pad(w, ((before, tail - before - w.shape[0]), (0, 0)))
    return dict(
        g_mix=norm_mix_g[l].reshape(1, d),
        w_a=_bf(win[:, :s1]), w_b=_bf(win[:, s1:s2]), w_c=_bf(win[:, s2:s3]),
        w_ga=_bf(win[:, s3:s3 + d]), w_gb=_bf(win[:, s3 + d:s3 + 2 * d]), w_gc=_bf(win[:, s3 + 2 * d:]),
        conv_w=conv_w[l],
        qg=jnp.tile(q_norm_g[l], n_h).reshape(1, wb), kg=jnp.tile(k_norm_g[l], n_h).reshape(1, wb),
        mu=mu_c[l].reshape(1, cc), w0=w0[l].reshape(1, wc), a0=a0[l].reshape(1, wc),
        k_k=k_k[l].reshape(1, wc), k_a=k_a[l].reshape(1, wc),
        ww2=_bf(zpad(w_w2[l], 0)), wa2=_bf(zpad(w_a2[l], LORA_W)), wg2=_bf(zpad(w_g2[l], LORA_W + LORA_A)),
        r_k=r_k[l].reshape(1, wc), lnx_g=lnx_g[l].reshape(1, wc), lnx_b=lnx_b[l].reshape(1, wc),
        w_br_a=_bf(w_br_a[l]), w_br_b=_bf(w_br_b[l]), w_br_c=_bf(w_br_c[l]), w_o=_bf(w_o[l]),
        g_mlp=norm_mlp_g[l].reshape(1, d), w_up=_bf(w_up[l]), w_dn=_bf(w_down[l]),
    )


def _pad_time(x, tp):
    return jnp.pad(x, ((0, 0), (0, tp - x.shape[1]), (0, 0)))


def _group_layer(x, w, consts, conv_state, shift_state, wkv_pairs, attn_fn, *, t_valid, bb, tt, tm):
    bg, tp, d = x.shape
    n = bg * tp
    wb = w["w_br_b"].shape[0]
    x2 = x.reshape(n, d)
    ya, conv_new = _mixer_a(x, w["g_mix"], w["w_a"], w["w_ga"], w["conv_w"], w["w_br_a"], conv_state,
                            t_valid=t_valid, bb=bb, tt=tt)
    if bb == 1:
        q16, k16, v16, k32, v32 = _proj_b(x, w["g_mix"], w["w_b"], w["qg"], w["kg"], consts["bd256"], tt=tt,
                                          t_out=t_valid)
    else:
        flat = _proj_b(x.reshape(1, n, d), w["g_mix"], w["w_b"], w["qg"], w["kg"], consts["bd256"], tt=tm, t_out=n)
        q16, k16, v16, k32, v32 = (t.reshape(bg, tp, wb) for t in flat)
        k32, v32 = k32[:, :t_valid], v32[:, :t_valid]
    o = attn_fn(q16, k16, v16)
    r, lw, k, v, a, b, g, shift_new = _proj_c(x, w["g_mix"], w["w_c"], w["mu"], w["w0"], w["a0"], w["k_k"], w["k_a"],
                                              w["ww2"], w["wa2"], w["wg2"], consts["bd256"], shift_state,
                                              t_valid=t_valid, bb=bb, tt=tt)
    ch = RWKV_CHUNK
    tpr = -(-tp // ch) * ch
    ops = [r, lw, k, v, a, b, g]
    if tpr != tp:
        ops = [_pad_time(t, tpr) for t in ops]
    yc, wkv_new = _rwkv(*ops, wkv_pairs, w["r_k"], w["lnx_g"], w["lnx_b"], consts["tril3"], consts["bd128"],
                        t_valid=t_valid, nb=_pick_tile(bg, RWKV_SEQS, 1))
    yc = yc[:, :tp]
    x_new = _merge(x2, ya.reshape(n, d), o.reshape(n, wb), yc.reshape(n, wb), w["g_mix"], w["w_gb"], w["w_gc"],
                   w["w_br_b"], w["w_br_c"], w["w_o"], w["g_mlp"], w["w_up"], w["w_dn"], tm=tm)
    return x_new.reshape(bg, tp, d), conv_new, shift_new, wkv_new, k32, v32


def kernel(x_prompt, x_sample, cache_k, cache_v, state_conv, state_shift, state_wkv, page_table, meta_tokens, norm_mix_g, w_in, conv_w, q_norm_g, k_norm_g, sb_bias, mu_c, w0, w_w2, a0, w_a2, w_g2, k_k, k_a, r_k, lnx_g, lnx_b, w_br_a, w_br_b, w_br_c, w_o, norm_mlp_g, w_up, w_down):
    bp, seq, d = x_prompt.shape
    db, dseq, _ = x_sample.shape
    depth = w_in.shape[0]
    wa = w_br_a.shape[1]
    wb = w_br_b.shape[1]
    wc = w_br_c.shape[1]
    cc = mu_c.shape[1]
    n_h = wb // HEAD_DIM
    n_hc = wc // HEAD_DIM
    n_pool, page = cache_k.shape[1], cache_k.shape[2]
    assert page == PAGE and cache_k.shape[3] == n_h and cache_k.shape[4] == HEAD_DIM
    assert n_h & (n_h - 1) == 0 and dseq * n_h <= LANES
    n_pages = page_table.shape[1]

    tp_valid = seq + N_META
    tp_p = -(-tp_valid // LANES) * LANES
    tp_s = -(-dseq // 8) * 8
    meta = jnp.broadcast_to(meta_tokens.astype(x_prompt.dtype)[None], (bp, N_META, d))
    xp = jnp.concatenate([meta, x_prompt, jnp.zeros((bp, tp_p - tp_valid, d), x_prompt.dtype)], axis=1)
    xs = _pad_time(x_sample, tp_s)

    ks = ATTN_KEYS
    tq = _pick_tile(tp_p, ATTN_QUERIES, ks)
    tri = np.tril(np.ones((ks, ks)))
    tril = np.tril(np.ones((RWKV_CHUNK, RWKV_CHUNK)))
    consts = dict(
        bd256=_block_diag_ones(2 * LANES, HEAD_DIM),
        bd128=_block_diag_ones(LANES, HEAD_DIM),
        tril3=jnp.asarray(np.concatenate([tril] * 3, axis=1), dtype=bf16),
        tri2=jnp.asarray(np.kron(np.eye(2), tri), dtype=bf16),
    )
    ck = jnp.transpose(cache_k, (0, 1, 3, 4, 2))
    cv = jnp.transpose(cache_v, (0, 1, 3, 4, 2))
    rows = dseq * n_h
    head_mask = jnp.asarray((np.arange(rows)[:, None] % n_h) == (np.arange(wb)[None, :] // HEAD_DIM), dtype=f32)
    n_group = _pick_tile(n_pages, 16, 1)

    tt_p = _pick_tile(tp_p, 528, 16)
    tm_p = _pick_tile(bp * tp_p, 512, 16)
    tm_s = _pick_tile(db * tp_s, 512, 8)

    zeros_conv = jnp.zeros((bp, CONV_W - 1, wa), f32)
    zeros_shift = jnp.zeros((bp, 1, cc), f32)
    zeros_wkv = jnp.zeros((bp, n_hc // 2, LANES, LANES), f32)

    outs = [[] for _ in range(10)]
    for l in range(depth):
        w = _layer_weights(l, d, wa, wb, wc, cc, norm_mix_g, w_in, conv_w, q_norm_g, k_norm_g, mu_c, w0, w_w2, a0,
                           w_a2, w_g2, k_k, k_a, r_k, lnx_g, lnx_b, w_br_a, w_br_b, w_br_c, w_o, norm_mlp_g, w_up,
                           w_down)
        bias2 = sb_bias[l].astype(f32) * LOG2E
        bias_pairs = jnp.repeat(bias2.reshape(n_h // 2, 1, 2), ks, axis=2)

        attn_p = lambda q, k, v: _attn_prompt(q, k, v, bias_pairs, consts["tri2"], tq=tq)
        xp, cp, sp, wp, kp, vp = _group_layer(xp, w, consts, zeros_conv, zeros_shift, zeros_wkv, attn_p,
                                              t_valid=tp_valid, bb=1, tt=tt_p, tm=tm_p)

        def attn_s(q, k, v, l=l):
            q4 = q[:, :dseq].reshape(db, dseq, 1, n_h, HEAD_DIM)
            qr = (q4 * jnp.eye(n_h, dtype=q.dtype)[None, None, :, :, None]).reshape(db, rows, wb)
            as_page = lambda t: jnp.pad(jnp.swapaxes(t[:, :dseq], 1, 2), ((0, 0), (0, 0), (0, LANES - dseq)))
            bias_rows = jnp.broadcast_to(jnp.tile(bias2, dseq)[:, None], (rows, LANES))
            o = _attn_sample(qr, as_page(k), as_page(v), bias_rows, head_mask, ck, cv, page_table, l,
                             n_group=n_group, n_heads=n_h)
            return _pad_time(o, tp_s)

        xs, cs, ss, ws, ks_new, vs_new = _group_layer(xs, w, consts, state_conv[l], state_shift[l][:, None, :],
                                                      _state_to_pairs(state_wkv[l]), attn_s,
                                                      t_valid=dseq, bb=db, tt=tp_s, tm=tm_s)
        heads = lambda t: t.reshape(t.shape[0], t.shape[1], n_h, HEAD_DIM)
        vals = (heads(kp), heads(vp), heads(ks_new), heads(vs_new),
                cp, cs, sp[:, 0], ss[:, 0], _pairs_to_state(wp, HEAD_DIM), _pairs_to_state(ws, HEAD_DIM))
        for lst, val in zip(outs, vals):
            lst.append(val)
    return (xp[:, N_META:tp_valid], xs[:, :dseq]) + tuple(jnp.stack(lst) for lst in outs)
```

```python
import functools

import numpy as np
import jax
import jax.numpy as jnp
from jax import lax
from jax.experimental import pallas as pl
from jax.experimental.pallas import tpu as pltpu

HEAD_DIM = 64
N_META = 16
CONV_W = 3
LORA_W = 64
LORA_A = 64
LORA_G = 128
NORM_EPS = 1e-6
LNX_EPS = 64e-5
LOG2E = 1.4426950408889634
LANES = 128
RWKV_CHUNK = 64
RWKV_SEQS = 2
ATTN_KEYS = 128
ATTN_QUERIES = 384
KEY_BLOCKS = 4
PAGE = 128
VMEM_LIMIT = 56 * 1024 * 1024

f32 = jnp.float32
bf16 = jnp.bfloat16


def _bf(x):
    return x.astype(bf16)


def _dot(a, b):
    return jnp.dot(a, b, preferred_element_type=f32)


def _dot_nt(a, b):
    return lax.dot_general(a, b, (((1,), (1,)), ((), ())), preferred_element_type=f32)


def _dot_tn(a, b):
    return lax.dot_general(a, b, (((0,), (0,)), ((), ())), preferred_element_type=f32)


def _split2(x):
    hi = x.astype(bf16)
    lo = (x - hi.astype(f32)).astype(bf16)
    return hi, lo


def _split3(x):
    hi = x.astype(bf16)
    r1 = x - hi.astype(f32)
    mid = r1.astype(bf16)
    lo = (r1 - mid.astype(f32)).astype(bf16)
    return hi, mid, lo


def _dot_sel(x, sel):
    hi, lo = _split2(x)
    return _dot(hi, sel) + _dot(lo, sel)


def _rms(x, g):
    ms = jnp.mean(x * x, axis=-1, keepdims=True)
    return x * lax.rsqrt(ms + NORM_EPS) * g


def _segsum(x, bd):
    w = bd.shape[0]
    parts = [_dot_sel(x[:, s:s + w], bd) for s in range(0, x.shape[1], w)]
    return parts[0] if len(parts) == 1 else jnp.concatenate(parts, axis=1)


def _neg_abs(z):
    bits = lax.bitcast_convert_type(z, jnp.int32) | jnp.int32(-2147483648)
    return lax.bitcast_convert_type(bits, f32)


def _softplus2(z2):
    return jnp.maximum(z2, 0.0) + jnp.log(1.0 + jnp.exp2(_neg_abs(z2))) * LOG2E


def _pick_tile(n, target, mult):
    best = None
    for t in range(mult, min(n, target) + 1, mult):
        if n % t == 0:
            best = t
    if best is None:
        assert n <= target, (n, target, mult)
        best = n
    return best


def _block_diag_ones(n, blk):
    i = np.arange(n)
    return jnp.asarray((i[:, None] // blk) == (i[None, :] // blk), dtype=bf16)


def _full(shape):
    nd = len(shape)
    return pl.BlockSpec(shape, lambda *_: (0,) * nd)


def _params(sem):
    return pltpu.CompilerParams(dimension_semantics=sem, vmem_limit_bytes=VMEM_LIMIT)


def _shift_down(v, k, prev, tpos):
    out = pltpu.roll(v, k, 0)
    for j in range(k):
        out = jnp.where(tpos == j, prev[j], out)
    return out


def _mixer_a_kernel(x_ref, g_ref, wa_ref, wg_ref, cw_ref, wbr_ref, st_ref, y_ref, ns_ref, carry_ref,
                    *, bb, tt, wa, t_valid):
    ti = pl.program_id(1)

    @pl.when(ti == 0)
    def _():
        carry_ref[...] = st_ref[...]

    d = x_ref.shape[-1]
    x = x_ref[...].reshape(bb * tt, d)
    xn = _bf(_rms(x, g_ref[...]))
    pa = _dot(xn, wa_ref[...])
    gb, gc, h = pa[:, :wa], pa[:, wa:2 * wa], pa[:, 2 * wa:]
    u = gc * h
    tpos = lax.broadcasted_iota(jnp.int32, (bb, tt, wa), 1).reshape(bb * tt, wa)
    carry = carry_ref[...]
    c0 = jnp.broadcast_to(carry[:, 0:1, :], (bb, tt, wa)).reshape(bb * tt, wa)
    c1 = jnp.broadcast_to(carry[:, 1:2, :], (bb, tt, wa)).reshape(bb * tt, wa)
    u1 = _shift_down(u, 1, (c1,), tpos)
    u2 = _shift_down(u, 2, (c0, c1), tpos)
    cw = cw_ref[...]
    conv = cw[0:1] * u2 + cw[1:2] * u1 + cw[2:3] * u
    ya = _dot(_bf(gb * conv), wbr_ref[...])
    gate = jax.nn.sigmoid(_dot(xn, wg_ref[...]))
    y_ref[...] = (gate * ya).reshape(bb, tt, d)
    u3 = u.reshape(bb, tt, wa)
    carry_ref[...] = u3[:, tt - 2:tt, :]

    lo = t_valid - 2
    @pl.when(ti == lo // tt)
    def _():
        ns_ref[...] = u3[:, lo % tt:lo % tt + 2, :]


def _mixer_a(x, g, w_a, w_ga, conv_w, w_br, state, *, t_valid, bb, tt):
    bg, tp, d = x.shape
    wa = w_br.shape[0]
    assert (t_valid - 2) // tt == (t_valid - 1) // tt
    grid = (bg // bb, tp // tt)
    kern = functools.partial(_mixer_a_kernel, bb=bb, tt=tt, wa=wa, t_valid=t_valid)
    return pl.pallas_call(
        kern,
        grid=grid,
        in_specs=[
            pl.BlockSpec((bb, tt, d), lambda b, t: (b, t, 0)),
            _full((1, d)), _full(w_a.shape), _full(w_ga.shape), _full(conv_w.shape), _full(w_br.shape),
            pl.BlockSpec((bb, 2, wa), lambda b, t: (b, 0, 0)),
        ],
        out_specs=[
            pl.BlockSpec((bb, tt, d), lambda b, t: (b, t, 0)),
            pl.BlockSpec((bb, 2, wa), lambda b, t: (b, 0, 0)),
        ],
        out_shape=[jax.ShapeDtypeStruct((bg, tp, d), f32), jax.ShapeDtypeStruct((bg, 2, wa), f32)],
        scratch_shapes=[pltpu.VMEM((bb, 2, wa), f32)],
        compiler_params=_params(("arbitrary", "arbitrary")),
        name="mixer_a",
    )(x, g, w_a, w_ga, conv_w, w_br, state)


def _proj_b_kernel(x_ref, g_ref, wb_ref, qg_ref, kg_ref, bd_ref, *rest, wb, transposed):
    q16_ref, k16_ref, v16_ref, k_ref, v_ref = rest[-5:]
    xn = _bf(_rms(x_ref[0], g_ref[...]))
    pb = _dot(xn, wb_ref[...])
    q, k, v = pb[:, :wb], pb[:, wb:2 * wb], pb[:, 2 * wb:]
    bd = bd_ref[...]
    inv = 1.0 / HEAD_DIM
    qn = q * lax.rsqrt(_segsum(q * q, bd) * inv + NORM_EPS) * qg_ref[...]
    kn = k * lax.rsqrt(_segsum(k * k, bd) * inv + NORM_EPS) * kg_ref[...]
    q16_ref[0] = _bf(qn * (LOG2E * HEAD_DIM ** -0.5))
    k16_ref[0] = _bf(kn)
    v16_ref[0] = _bf(v)
    if transposed:
        k_ref[0, 0] = kn.T
        v_ref[0, 0] = v.T
    else:
        k_ref[0] = kn
        v_ref[0] = v


def _proj_b(x, g, w_b, qg, kg, bd, *, tt, t_out, stacked=None):
    bg, tp, d = x.shape
    wb = w_b.shape[1] // 3
    blk = lambda b, t: (b, t, 0)
    in_specs = [pl.BlockSpec((1, tt, d), blk), _full((1, d)), _full(w_b.shape), _full((1, wb)), _full((1, wb)),
                _full(bd.shape)]
    args = [x, g, w_b, qg, kg, bd]
    aliases = {}
    if stacked is None:
        kv_shape = jax.ShapeDtypeStruct((bg, t_out, wb), f32)
        kv_spec = pl.BlockSpec((1, tt, wb), blk)
    else:
        layer, depth, k_prev, v_prev = stacked
        kv_shape = jax.ShapeDtypeStruct((depth, bg, wb, t_out), f32)
        kv_spec = pl.BlockSpec((1, 1, wb, tt), lambda b, t: (layer, b, 0, t))
        if k_prev is not None:
            aliases = {len(args): 3, len(args) + 1: 4}
            args += [k_prev, v_prev]
            in_specs += [pl.BlockSpec(memory_space=pl.ANY)] * 2
    kern = functools.partial(_proj_b_kernel, wb=wb, transposed=stacked is not None)
    return pl.pallas_call(
        kern,
        grid=(bg, tp // tt),
        in_specs=in_specs,
        out_specs=[pl.BlockSpec((1, tt, wb), blk)] * 3 + [kv_spec] * 2,
        out_shape=[jax.ShapeDtypeStruct((bg, tp, wb), bf16)] * 3 + [kv_shape] * 2,
        input_output_aliases=aliases,
        compiler_params=_params(("arbitrary", "arbitrary")),
        name="proj_b",
    )(*args)


def _attn_prompt_kernel(bias_ref, q_ref, k_ref, v_ref, tri_ref, o_ref, *, tq, ks):
    qi = pl.program_id(2)
    q2 = q_ref[0]
    bias = bias_ref[0]
    tri2 = tri_ref[...]
    n_sub = tq // ks
    first = lax.broadcasted_iota(jnp.int32, (ks, LANES), 1) < HEAD_DIM
    row = lax.broadcasted_iota(jnp.int32, (tq, 2 * ks), 0)
    key = lax.broadcasted_iota(jnp.int32, (tq, 2 * ks), 1) & (ks - 1)

    def stack_heads(x):
        zero = jnp.zeros_like(x)
        return jnp.concatenate([jnp.where(first, x, zero), jnp.where(first, zero, x)], axis=0)

    def sub_block(start, c, acc, mask=None, lo=0):
        n = tq - lo
        k_st = stack_heads(k_ref[0, pl.ds(start, ks), :])
        v_st = stack_heads(v_ref[0, pl.ds(start, ks), :])
        z = _dot_nt(q2[lo:], k_st) + bias
        nl = _softplus2(z)
        if mask is not None:
            nl = jnp.where(mask[lo:], nl, 0.0)
        incl = _dot(_bf(nl), tri2)
        p = jnp.exp2(z - incl - c[lo:])
        if mask is not None:
            p = jnp.where(mask[lo:], p, 0.0)
        acc_lo = acc[lo:] + _dot(_bf(p), v_st)
        c_lo = c[lo:] + jnp.concatenate([jnp.broadcast_to(incl[:, 0:1], (n, ks)),
                                         jnp.broadcast_to(incl[:, ks:ks + 1], (n, ks))], axis=1)
        if lo:
            c_lo = jnp.concatenate([c[:lo], c_lo], axis=0)
            acc_lo = jnp.concatenate([acc[:lo], acc_lo], axis=0)
        return c_lo, acc_lo

    def key_blocks(start, n, c, acc):
        for j in reversed(range(n)):
            c, acc = sub_block(pl.multiple_of(start + j * ks, ks), c, acc)
        return c, acc

    left = qi % KEY_BLOCKS

    def head(extra):
        c = jnp.zeros((tq, 2 * ks), f32)
        acc = jnp.zeros((tq, LANES), f32)
        for j in reversed(range(n_sub)):
            c, acc = sub_block(pl.multiple_of(qi * tq + j * ks, ks), c, acc, (key + j * ks) < row, j * ks)
        return key_blocks((qi - extra) * tq, extra * n_sub, c, acc)

    def pick(lo, hi):
        if lo == hi:
            return head(lo)
        mid = (lo + hi + 1) // 2
        return lax.cond(left >= mid, lambda: pick(mid, hi), lambda: pick(lo, mid - 1))

    c, acc = pick(0, KEY_BLOCKS - 1)

    def body(n, carry):
        return key_blocks((qi - left - KEY_BLOCKS * (n + 1)) * tq, KEY_BLOCKS * n_sub, carry[0], carry[1])

    c, acc = lax.fori_loop(0, qi // KEY_BLOCKS, body, (c, acc))
    o_ref[0] = acc


def _attn_prompt(q16, k16, v16, bias_pairs, tri2, *, tq):
    bg, tp, wb = q16.shape
    ks = tri2.shape[0] // 2
    grid = (bg, wb // LANES, tp // tq)
    kern = functools.partial(_attn_prompt_kernel, tq=tq, ks=ks)
    return pl.pallas_call(
        kern,
        grid=grid,
        in_specs=[
            pl.BlockSpec((1, 1, 2 * ks), lambda b, h, i: (h, 0, 0)),
            pl.BlockSpec((1, tq, LANES), lambda b, h, i: (b, i, h)),
            pl.BlockSpec((1, tp, LANES), lambda b, h, i: (b, 0, h)),
            pl.BlockSpec((1, tp, LANES), lambda b, h, i: (b, 0, h)),
            _full(tri2.shape),
        ],
        out_specs=pl.BlockSpec((1, tq, LANES), lambda b, h, i: (b, i, h)),
        out_shape=jax.ShapeDtypeStruct((bg, tp, wb), f32),
        compiler_params=_params(("arbitrary", "arbitrary", "arbitrary")),
        name="attn_prompt",
    )(bias_pairs, q16, k16, v16, tri2)


def _scan_lanes(x):
    n = x.shape[1]
    lane = lax.broadcasted_iota(jnp.int32, x.shape, 1)
    d = 1
    while d < n:
        x = x + jnp.where(lane < n - d, pltpu.roll(x, n - d, 1), 0.0)
        d *= 2
    return x


def _attn_sample_kernel(pt_ref, q_ref, kn_ref, vn_ref, bias_ref, hm_ref, *rest, n_group, n_heads):
    kp_refs = rest[:n_group]
    vp_refs = rest[n_group:2 * n_group]
    o_ref, acc_ref, c_ref = rest[2 * n_group:]
    si = pl.program_id(1)
    q = q_ref[0]
    rows, w = q.shape
    bias = bias_ref[...]

    def block(kt, vt, c, acc, mask):
        reps = kt.shape[1] // LANES
        z = _dot(q, kt) + jnp.concatenate([bias] * reps, axis=1)
        nl = _softplus2(z)
        if mask is not None:
            nl = jnp.where(mask, nl, 0.0)
        incl = _scan_lanes(nl)
        p = jnp.exp2(z - incl - jnp.concatenate([c] * reps, axis=1))
        if mask is not None:
            p = jnp.where(mask, p, 0.0)
        return c + incl[:, 0:1], acc + _dot_nt(_bf(p), vt)

    @pl.when(si == 0)
    def _():
        trow = lax.broadcasted_iota(jnp.int32, (rows, LANES), 0) // n_heads
        pos = lax.broadcasted_iota(jnp.int32, (rows, LANES), 1)
        c, acc = block(kn_ref[0], vn_ref[0], jnp.zeros(c_ref.shape, f32), jnp.zeros(acc_ref.shape, f32), pos < trow)
        c_ref[...] = c
        acc_ref[...] = acc

    def pages(refs):
        return jnp.concatenate([_bf(r[0, 0].reshape(w, PAGE)) for r in reversed(refs)], axis=1)

    c, acc = block(pages(kp_refs), pages(vp_refs), c_ref[...], acc_ref[...], None)
    c_ref[...] = c
    acc_ref[...] = acc

    @pl.when(si == pl.num_programs(1) - 1)
    def _():
        own = jnp.where(hm_ref[...] > 0, acc, 0.0).reshape(rows // n_heads, n_heads, w)
        o_ref[0] = jnp.sum(own, axis=1)


def _attn_sample(q_rows, k_new, v_new, bias_rows, head_mask, cache_k, cache_v, page_table, layer, *, n_group,
                 n_heads):
    db, rows, wb = q_rows.shape
    n_pages = page_table.shape[1]
    n_steps = n_pages // n_group
    page_blk = (1, 1) + cache_k.shape[2:]

    def page_map(g):
        def f(b, s, pt):
            return (layer, pt[b, n_pages - 1 - (s * n_group + g)], 0, 0, 0)
        return f

    page_specs = [pl.BlockSpec(page_blk, page_map(g)) for g in range(n_group)]
    seq = lambda b, s, pt: (b, 0, 0)
    const2 = lambda b, s, pt: (0, 0)
    t_rows = rows // n_heads
    kern = functools.partial(_attn_sample_kernel, n_group=n_group, n_heads=n_heads)
    gs = pltpu.PrefetchScalarGridSpec(
        num_scalar_prefetch=1,
        grid=(db, n_steps),
        in_specs=[
            pl.BlockSpec((1, rows, wb), seq),
            pl.BlockSpec((1,) + k_new.shape[1:], seq),
            pl.BlockSpec((1,) + v_new.shape[1:], seq),
            pl.BlockSpec(bias_rows.shape, const2),
            pl.BlockSpec(head_mask.shape, const2),
        ] + page_specs + page_specs,
        out_specs=pl.BlockSpec((1, t_rows, wb), seq),
        scratch_shapes=[pltpu.VMEM((rows, wb), f32), pltpu.VMEM((rows, LANES), f32)],
    )
    return pl.pallas_call(
        kern,
        grid_spec=gs,
        out_shape=jax.ShapeDtypeStruct((db, t_rows, wb), f32),
        compiler_params=_params(("arbitrary", "arbitrary")),
        name="attn_sample",
    )(page_table, q_rows, k_new, v_new, bias_rows, head_mask, *([cache_k] * n_group), *([cache_v] * n_group))


def _proj_c_kernel(x_ref, g_ref, wc_ref, mu_ref, w0_ref, a0_ref, kk_ref, ka_ref, ww2_ref, wa2_ref, wg2_ref, bd_ref,
                   st_ref, r_ref, lw_ref, k_ref, v_ref, a_ref, b_ref, gg_ref, ns_ref, carry_ref,
                   *, bb, tt, wc, t_valid):
    ti = pl.program_id(1)

    @pl.when(ti == 0)
    def _():
        carry_ref[...] = st_ref[...]

    d = x_ref.shape[-1]
    cc = wc_ref.shape[1]
    x = x_ref[...].reshape(bb * tt, d)
    xn = _bf(_rms(x, g_ref[...]))
    pc = _dot(xn, wc_ref[...])
    tpos = lax.broadcasted_iota(jnp.int32, (bb, tt, cc), 1).reshape(bb * tt, cc)
    cprev = jnp.broadcast_to(carry_ref[...], (bb, tt, cc)).reshape(bb * tt, cc)
    prev = _shift_down(pc, 1, (cprev,), tpos)
    pcs = pc + (prev - pc) * mu_ref[...]
    r, kc, vc, lo = pcs[:, :wc], pcs[:, wc:2 * wc], pcs[:, 2 * wc:3 * wc], pcs[:, 3 * wc:]
    w_log = -jax.nn.softplus(-(w0_ref[...] + _dot(_bf(jnp.tanh(lo)), ww2_ref[...]))) - 0.5
    a = jax.nn.sigmoid(a0_ref[...] + _dot(_bf(lo), wa2_ref[...]))
    g = _dot(_bf(jax.nn.sigmoid(lo)), wg2_ref[...])
    kk = kc * kk_ref[...]
    kk = kk / jnp.maximum(jnp.sqrt(_segsum(kk * kk, bd_ref[...])), 1e-12)
    kmod = kc * (1.0 + (a - 1.0) * ka_ref[...])
    shp = (bb, tt, wc)
    lw_ref[...] = (-jnp.exp(w_log)).reshape(shp)
    for ref, val in ((r_ref, r), (k_ref, kmod), (v_ref, vc), (a_ref, -kk), (b_ref, kk * a), (gg_ref, g)):
        ref[...] = val.astype(ref.dtype).reshape(shp)
    pc3 = pc.reshape(bb, tt, cc)
    carry_ref[...] = pc3[:, tt - 1:tt, :]

    last = t_valid - 1
    @pl.when(ti == last // tt)
    def _():
        ns_ref[...] = pc3[:, last % tt:last % tt + 1, :]


def _proj_c(x, g, w_c, mu, w0, a0, k_k, k_a, ww2, wa2, wg2, bd, state, *, t_valid, bb, tt):
    bg, tp, d = x.shape
    cc = w_c.shape[1]
    wc = w0.shape[1]
    grid = (bg // bb, tp // tt)
    blk = lambda b, t: (b, t, 0)
    kern = functools.partial(_proj_c_kernel, bb=bb, tt=tt, wc=wc, t_valid=t_valid)
    act = jax.ShapeDtypeStruct((bg, tp, wc), bf16 if tt % 16 == 0 else f32)
    lw_act = jax.ShapeDtypeStruct((bg, tp, wc), f32)
    return pl.pallas_call(
        kern,
        grid=grid,
        in_specs=[
            pl.BlockSpec((bb, tt, d), blk),
            _full((1, d)), _full(w_c.shape), _full((1, cc)), _full((1, wc)), _full((1, wc)), _full((1, wc)),
            _full((1, wc)), _full(ww2.shape), _full(wa2.shape), _full(wg2.shape), _full(bd.shape),
            pl.BlockSpec((bb, 1, cc), lambda b, t: (b, 0, 0)),
        ],
        out_specs=[pl.BlockSpec((bb, tt, wc), blk)] * 7 + [pl.BlockSpec((bb, 1, cc), lambda b, t: (b, 0, 0))],
        out_shape=[act, lw_act] + [act] * 5 + [jax.ShapeDtypeStruct((bg, 1, cc), f32)],
        scratch_shapes=[pltpu.VMEM((bb, 1, cc), f32)],
        compiler_params=_params(("arbitrary", "arbitrary")),
        name="proj_c",
    )(x, g, w_c, mu, w0, a0, k_k, k_a, ww2, wa2, wg2, bd, state)


def _rwkv_kernel(r_ref, lw_ref, k_ref, v_ref, a_ref, b_ref, g_ref, s0_ref, rk_ref, lng_ref, lnb_ref, tril3_ref,
                 bd_ref, y_ref, so_ref, s_ref, *, t_valid, ch, n_pairs, nb):
    ci = pl.program_id(1)

    @pl.when(ci == 0)
    def _():
        s_ref[...] = s0_ref[...]

    n2 = 2 * ch
    tpos = lax.broadcasted_iota(jnp.int32, (ch, LANES), 0) + ci * ch
    valid = tpos < t_valid
    first = lax.broadcasted_iota(jnp.int32, (ch, LANES), 1) < HEAD_DIM
    row = lax.broadcasted_iota(jnp.int32, (n2, n2), 0)
    col = lax.broadcasted_iota(jnp.int32, (n2, n2), 1)
    strict = row > col
    incl = row >= col
    eye = (row == col).astype(f32)
    tril3 = tril3_ref[...]
    bd = bd_ref[...]
    zb = jnp.zeros((ch, LANES), bf16)

    def stack(x):
        xb = _bf(x)
        return jnp.concatenate([jnp.where(first, xb, zb), jnp.where(first, zb, xb)], axis=0)

    chains = [(i, p) for i in range(nb) for p in range(n_pairs)]
    sls = [slice(p * LANES, (p + 1) * LANES) for _, p in chains]
    ld = lambda ref: [ref[i, :, sl].astype(f32) for (i, _), sl in zip(chains, sls)]
    lw = [jnp.where(valid, x, 0.0) for x in ld(lw_ref)]
    cum = [_dot(tril3, jnp.concatenate(_split3(x), axis=0)) for x in lw]
    e_in = [jnp.exp(c) for c in cum]
    e_neg = [jnp.exp(-c) for c in cum]
    r, k, v, a, b, g = ld(r_ref), ld(k_ref), ld(v_ref), ld(a_ref), ld(b_ref), ld(g_ref)
    v2 = [stack(x) for x in v]
    ar = [jnp.concatenate([stack(aa * jnp.exp(c - x)), stack(rr * ei)], axis=0)
          for aa, c, x, rr, ei in zip(a, cum, lw, r, e_in)]
    bk = [jnp.concatenate([stack(jnp.where(valid, bb, 0.0) * en), stack(jnp.where(valid, kk, 0.0) * en)], axis=0)
          for bb, kk, en in zip(b, k, e_neg)]
    gm = [_dot_nt(x, y) for x, y in zip(ar, bk)]
    l_ab = [jnp.where(strict, g[:n2, :n2], 0.0) for g in gm]
    lm = [_bf(jnp.concatenate([jnp.where(strict, g[:n2, n2:], 0.0), jnp.where(incl, g[n2:, n2:], 0.0)], axis=0))
          for g in gm]
    m_rb = [_bf(jnp.where(incl, g[n2:, :n2], 0.0)) for g in gm]
    part_v = [_dot(x, y) for x, y in zip(lm, v2)]
    tk = [eye + x for x in l_ab]
    pk = [_dot(_bf(x), _bf(x)) for x in l_ab]
    levels = int(np.log2(ch)) - 1
    for i in range(levels):
        pb = [_bf(x) for x in pk]
        if i < levels - 1:
            both = [_dot(x, jnp.concatenate([x, _bf(t)], axis=1)) for x, t in zip(pb, tk)]
            pk = [x[:, :n2] for x in both]
            tk = [t + x[:, n2:] for t, x in zip(tk, both)]
        else:
            tk = [t + _dot(x, _bf(t)) for t, x in zip(tk, pb)]
    s = [s_ref[i, p] for i, p in chains]
    part_s = [_dot_nt(x, _bf(y)) for x, y in zip(ar, s)]
    u = [_bf(_dot(_bf(t), _bf(ps[:n2] + pv[:n2]))) for t, ps, pv in zip(tk, part_s, part_v)]
    y2 = [ps[n2:] + pv[n2:] + _dot(m, uu) for ps, pv, m, uu in zip(part_s, part_v, m_rb, u)]
    s_new = [(ss + _dot_tn(jnp.concatenate([uu, vv], axis=0), y)) * ei[ch - 1:ch, :]
             for ss, uu, vv, y, ei in zip(s, u, v2, bk, e_in)]
    y = [x[:ch] + x[ch:] for x in y2]
    inv = 1.0 / HEAD_DIM
    mu = [_dot(_bf(x), bd) * inv for x in y]
    dy = [x - m for x, m in zip(y, mu)]
    var = [_dot(_bf(x * x), bd) * inv for x in dy]
    bonus = [_dot(_bf(rr * kk * rk_ref[:, sl]), bd) * vv for rr, kk, vv, sl in zip(r, k, v, sls)]
    outs = [(x * lax.rsqrt(vr + LNX_EPS) * lng_ref[:, sl] + lnb_ref[:, sl] + bo) * gg
            for x, vr, bo, sl, gg in zip(dy, var, bonus, sls, g)]
    for (i, p), x in zip(chains, s_new):
        s_ref[i, p] = x
    for i in range(nb):
        y_ref[i] = jnp.concatenate(outs[i * n_pairs:(i + 1) * n_pairs], axis=1)

    @pl.when(ci == pl.num_programs(1) - 1)
    def _():
        so_ref[...] = s_ref[...]


def _rwkv(r, lw, k, v, a, b, g, s0, r_k, lnx_g, lnx_b, tril3, bd, *, t_valid, nb):
    bg, tp, wc = r.shape
    ch = RWKV_CHUNK
    n_pairs = wc // LANES
    grid = (bg // nb, tp // ch)
    blk = lambda b_, c: (b_, c, 0)
    sblk = lambda b_, c: (b_, 0, 0, 0)
    kern = functools.partial(_rwkv_kernel, t_valid=t_valid, ch=ch, n_pairs=n_pairs, nb=nb)
    act = pl.BlockSpec((nb, ch, wc), blk)
    st = pl.BlockSpec((nb, n_pairs, LANES, LANES), sblk)
    return pl.pallas_call(
        kern,
        grid=grid,
        in_specs=[act] * 7 + [st, _full((1, wc)), _full((1, wc)), _full((1, wc)), _full(tril3.shape),
                              _full(bd.shape)],
        out_specs=[act, st],
        out_shape=[jax.ShapeDtypeStruct((bg, tp, wc), f32), jax.ShapeDtypeStruct(s0.shape, f32)],
        scratch_shapes=[pltpu.VMEM((nb, n_pairs, LANES, LANES), f32)],
        compiler_params=_params(("arbitrary", "arbitrary")),
        name="rwkv",
    )(r, lw, k, v, a, b, g, s0, r_k, lnx_g, lnx_b, tril3, bd)


def _state_to_pairs(s):
    b, h, n, _ = s.shape
    s = s.reshape(b, h // 2, 2, n, n)
    eye2 = jnp.eye(2, dtype=s.dtype)
    bd = s[:, :, :, :, None, :] * eye2[None, None, :, None, :, None]
    return bd.reshape(b, h // 2, 2 * n, 2 * n)


def _pairs_to_state(sp, n):
    b, p = sp.shape[:2]
    s = sp.reshape(b, p, 2, n, 2, n)
    s = jnp.stack([s[:, :, 0, :, 0, :], s[:, :, 1, :, 1, :]], axis=2)
    return s.reshape(b, 2 * p, n, n)


def _merge_kernel(x_ref, ya_ref, o_ref, yc_ref, g_ref, wgb_ref, wgc_ref, wbb_ref, wbc_ref, wo_ref, g2_ref, wup_ref,
                  wdn_ref, out_ref):
    x = x_ref[...]
    xn = _bf(_rms(x, g_ref[...]))
    yb = _dot(_bf(o_ref[...]), wbb_ref[...])
    yc = _dot(_bf(yc_ref[...]), wbc_ref[...])
    mixed = ya_ref[...] + jax.nn.sigmoid(_dot(xn, wgb_ref[...])) * yb + jax.nn.sigmoid(_dot(xn, wgc_ref[...])) * yc
    x1 = x + _dot(_bf(mixed), wo_ref[...])
    xn2 = _bf(_rms(x1, g2_ref[...]))
    hid = jnp.square(jnp.maximum(_dot(xn2, wup_ref[...]), 0.0))
    out_ref[...] = x1 + _dot(_bf(hid), wdn_ref[...])


def _merge(x2, ya2, o2, yc2, g, w_gb, w_gc, w_bb, w_bc, w_o, g2, w_up, w_dn, *, tm):
    n, d = x2.shape
    wb = o2.shape[1]
    row = lambda i: (i, 0)
    ws = [w_gb, w_gc, w_bb, w_bc, w_o]
    return pl.pallas_call(
        _merge_kernel,
        grid=(n // tm,),
        in_specs=[pl.BlockSpec((tm, d), row), pl.BlockSpec((tm, d), row), pl.BlockSpec((tm, wb), row),
                  pl.BlockSpec((tm, wb), row), _full((1, d))] + [_full(w.shape) for w in ws]
                 + [_full((1, d)), _full(w_up.shape), _full(w_dn.shape)],
        out_specs=pl.BlockSpec((tm, d), row),
        out_shape=jax.ShapeDtypeStruct((n, d), f32),
        compiler_params=_params(("arbitrary",)),
        name="merge_mlp",
    )(x2, ya2, o2, yc2, g, *ws, g2, w_up, w_dn)


def _layer_weights(l, d, wa, wb, wc, cc, norm_mix_g, w_in, conv_w, q_norm_g, k_norm_g, mu_c, w0, w_w2, a0, w_a2, w_g2,
                   k_k, k_a, r_k, lnx_g, lnx_b, w_br_a, w_br_b, w_br_c, w_o, norm_mlp_g, w_up, w_down):
    s1 = 3 * wa
    s2 = s1 + 3 * wb
    s3 = s2 + cc
    win = w_in[l]
    n_h = wb // HEAD_DIM
    tail = cc - 3 * wc
    zpad = lambda w, before: jnp.pad(w, ((before, tail - before - w.shape[0]), (0, 0)))
    return dict(
        g_mix=norm_mix_g[l].reshape(1, d),
        w_a=_bf(win[:, :s1]), w_b=_bf(win[:, s1:s2]), w_c=_bf(win[:, s2:s3]),
        w_ga=_bf(win[:, s3:s3 + d]), w_gb=_bf(win[:, s3 + d:s3 + 2 * d]), w_gc=_bf(win[:, s3 + 2 * d:]),
        conv_w=conv_w[l],
        qg=jnp.tile(q_norm_g[l], n_h).reshape(1, wb), kg=jnp.tile(k_norm_g[l], n_h).reshape(1, wb),
        mu=mu_c[l].reshape(1, cc), w0=w0[l].reshape(1, wc), a0=a0[l].reshape(1, wc),
        k_k=k_k[l].reshape(1, wc), k_a=k_a[l].reshape(1, wc),
        ww2=_bf(zpad(w_w2[l], 0)), wa2=_bf(zpad(w_a2[l], LORA_W)), wg2=_bf(zpad(w_g2[l], LORA_W + LORA_A)),
        r_k=r_k[l].reshape(1, wc), lnx_g=lnx_g[l].reshape(1, wc), lnx_b=lnx_b[l].reshape(1, wc),
        w_br_a=_bf(w_br_a[l]), w_br_b=_bf(w_br_b[l]), w_br_c=_bf(w_br_c[l]), w_o=_bf(w_o[l]),
        g_mlp=norm_mlp_g[l].reshape(1, d), w_up=_bf(w_up[l]), w_dn=_bf(w_down[l]),
    )


def _pad_time(x, tp):
    return jnp.pad(x, ((0, 0), (0, tp - x.shape[1]), (0, 0)))


def _group_layer(x, w, consts, conv_state, shift_state, wkv_pairs, attn_fn, *, t_valid, bb, tt, tm, stacked=None):
    bg, tp, d = x.shape
    n = bg * tp
    wb = w["w_br_b"].shape[0]
    x2 = x.reshape(n, d)
    ya, conv_new = _mixer_a(x, w["g_mix"], w["w_a"], w["w_ga"], w["conv_w"], w["w_br_a"], conv_state,
                            t_valid=t_valid, bb=bb, tt=tt)
    if bb == 1:
        q16, k16, v16, k32, v32 = _proj_b(x, w["g_mix"], w["w_b"], w["qg"], w["kg"], consts["bd256"],
                                          tt=_pick_tile(tp, tt, LANES), t_out=t_valid, stacked=stacked)
    else:
        flat = _proj_b(x.reshape(1, n, d), w["g_mix"], w["w_b"], w["qg"], w["kg"], consts["bd256"], tt=tm, t_out=n)
        q16, k16, v16, k32, v32 = (t.reshape(bg, tp, wb) for t in flat)
        k32, v32 = k32[:, :t_valid], v32[:, :t_valid]
    o = attn_fn(q16, k16, v16)
    r, lw, k, v, a, b, g, shift_new = _proj_c(x, w["g_mix"], w["w_c"], w["mu"], w["w0"], w["a0"], w["k_k"], w["k_a"],
                                              w["ww2"], w["wa2"], w["wg2"], consts["bd256"], shift_state,
                                              t_valid=t_valid, bb=bb, tt=tt)
    ch = RWKV_CHUNK
    tpr = -(-tp // ch) * ch
    ops = [r, lw, k, v, a, b, g]
    if tpr != tp:
        ops = [_pad_time(t, tpr) for t in ops]
    yc, wkv_new = _rwkv(*ops, wkv_pairs, w["r_k"], w["lnx_g"], w["lnx_b"], consts["tril3"], consts["bd128"],
                        t_valid=t_valid, nb=_pick_tile(bg, RWKV_SEQS, 1))
    yc = yc[:, :tp]
    x_new = _merge(x2, ya.reshape(n, d), o.reshape(n, wb), yc.reshape(n, wb), w["g_mix"], w["w_gb"], w["w_gc"],
                   w["w_br_b"], w["w_br_c"], w["w_o"], w["g_mlp"], w["w_up"], w["w_dn"], tm=tm)
    return x_new.reshape(bg, tp, d), conv_new, shift_new, wkv_new, k32, v32


def kernel(x_prompt, x_sample, cache_k, cache_v, state_conv, state_shift, state_wkv, page_table, meta_tokens, norm_mix_g, w_in, conv_w, q_norm_g, k_norm_g, sb_bias, mu_c, w0, w_w2, a0, w_a2, w_g2, k_k, k_a, r_k, lnx_g, lnx_b, w_br_a, w_br_b, w_br_c, w_o, norm_mlp_g, w_up, w_down):
    bp, seq, d = x_prompt.shape
    db, dseq, _ = x_sample.shape
    depth = w_in.shape[0]
    wa = w_br_a.shape[1]
    wb = w_br_b.shape[1]
    wc = w_br_c.shape[1]
    cc = mu_c.shape[1]
    n_h = wb // HEAD_DIM
    n_hc = wc // HEAD_DIM
    n_pool, page = cache_k.shape[1], cache_k.shape[2]
    assert page == PAGE and cache_k.shape[3] == n_h and cache_k.shape[4] == HEAD_DIM
    assert n_h & (n_h - 1) == 0 and dseq * n_h <= LANES
    n_pages = page_table.shape[1]

    tp_valid = seq + N_META
    tp_p = -(-tp_valid // LANES) * LANES
    tp_s = -(-dseq // 8) * 8
    meta = jnp.broadcast_to(meta_tokens.astype(x_prompt.dtype)[None], (bp, N_META, d))
    xp = lax.dynamic_update_slice(jnp.pad(x_prompt, ((0, 0), (N_META, tp_p - tp_valid), (0, 0))), meta, (0, 0, 0))
    xs = _pad_time(x_sample, tp_s)

    ks = ATTN_KEYS
    tq = _pick_tile(tp_p, ATTN_QUERIES, ks)
    tri = np.tril(np.ones((ks, ks)))
    tril = np.tril(np.ones((RWKV_CHUNK, RWKV_CHUNK)))
    consts = dict(
        bd256=_block_diag_ones(2 * LANES, HEAD_DIM),
        bd128=_block_diag_ones(LANES, HEAD_DIM),
        tril3=jnp.asarray(np.concatenate([tril] * 3, axis=1), dtype=bf16),
        tri2=jnp.asarray(np.kron(np.eye(2), tri), dtype=bf16),
    )
    ck = jnp.transpose(cache_k, (0, 1, 3, 4, 2))
    cv = jnp.transpose(cache_v, (0, 1, 3, 4, 2))
    rows = dseq * n_h
    head_mask = jnp.asarray((np.arange(rows)[:, None] % n_h) == (np.arange(wb)[None, :] // HEAD_DIM), dtype=f32)
    n_group = _pick_tile(n_pages, 16, 1)

    tt_p = _pick_tile(tp_p, 528, 16)
    tm_p = _pick_tile(bp * tp_p, 512, 16)
    tm_s = _pick_tile(db * tp_s, 512, 8)

    zeros_conv = jnp.zeros((bp, CONV_W - 1, wa), f32)
    zeros_shift = jnp.zeros((bp, 1, cc), f32)
    zeros_wkv = jnp.zeros((bp, n_hc // 2, LANES, LANES), f32)

    outs = [[] for _ in range(8)]
    kp = vp = None
    for l in range(depth):
        w = _layer_weights(l, d, wa, wb, wc, cc, norm_mix_g, w_in, conv_w, q_norm_g, k_norm_g, mu_c, w0, w_w2, a0,
                           w_a2, w_g2, k_k, k_a, r_k, lnx_g, lnx_b, w_br_a, w_br_b, w_br_c, w_o, norm_mlp_g, w_up,
                           w_down)
        bias2 = sb_bias[l].astype(f32) * LOG2E
        bias_pairs = jnp.repeat(bias2.reshape(n_h // 2, 1, 2), ks, axis=2)

        attn_p = lambda q, k, v: _attn_prompt(q, k, v, bias_pairs, consts["tri2"], tq=tq)
        xp, cp, sp, wp, kp, vp = _group_layer(xp, w, consts, zeros_conv, zeros_shift, zeros_wkv, attn_p,
                                              t_valid=tp_valid, bb=1, tt=tt_p, tm=tm_p, stacked=(l, depth, kp, vp))

        def attn_s(q, k, v, l=l):
            q4 = q[:, :dseq].reshape(db, dseq, 1, n_h, HEAD_DIM)
            qr = (q4 * jnp.eye(n_h, dtype=q.dtype)[None, None, :, :, None]).reshape(db, rows, wb)
            as_page = lambda t: jnp.pad(jnp.swapaxes(t[:, :dseq], 1, 2), ((0, 0), (0, 0), (0, LANES - dseq)))
            bias_rows = jnp.broadcast_to(jnp.tile(bias2, dseq)[:, None], (rows, LANES))
            o = _attn_sample(qr, as_page(k), as_page(v), bias_rows, head_mask, ck, cv, page_table, l,
                             n_group=n_group, n_heads=n_h)
            return _pad_time(o, tp_s)

        xs, cs, ss, ws, ks_new, vs_new = _group_layer(xs, w, consts, state_conv[l], state_shift[l][:, None, :],
                                                      _state_to_pairs(state_wkv[l]), attn_s,
                                                      t_valid=dseq, bb=db, tt=tp_s, tm=tm_s)
        heads = lambda t: t.reshape(t.shape[0], t.shape[1], n_h, HEAD_DIM)
        vals = (heads(ks_new), heads(vs_new),
                cp, cs, sp[:, 0], ss[:, 0], _pairs_to_state(wp, HEAD_DIM), _pairs_to_state(ws, HEAD_DIM))
        for lst, val in zip(outs, vals):
            lst.append(val)
    kv = [t.reshape(depth, bp, n_h, HEAD_DIM, tp_valid).transpose(0, 1, 4, 2, 3) for t in (kp, vp)]
    return (xp[:, N_META:tp_valid], xs[:, :dseq], *kv) + tuple(jnp.stack(lst) for lst in outs)
```
